```python
import jax, jax.numpy as jnp
from jax import lax
import numpy as np

D_MODEL = 2048
BATCH = 2
SEQ = 4096
DEPTH = 2
DEC_BATCH = 128
DEC_SEQ = 1
PAST_LEN = 2048
PAGE_SIZE = 128

BRANCH_WIDTH = D_MODEL // 2
HEAD_DIM = 64
N_HEADS = BRANCH_WIDTH // HEAD_DIM
KV_HEADS = 4
Q_PER_KV = N_HEADS // KV_HEADS
KV_WIDTH = KV_HEADS * HEAD_DIM
ROT_DIM = HEAD_DIM // 4
ROPE_THETA = 500000.0
CMP_BLOCK = 32
CMP_STRIDE = 16
CMP_RATIO = CMP_BLOCK // CMP_STRIDE
SLC_BLOCK = 64
N_SELECT = 16
WINDOW = 512
Q_BLOCK = 128
GMLP_WIDTH = BRANCH_WIDTH
GMLP_GROUPS = 8
CHUNK = 128
CONV_WIDTH = BRANCH_WIDTH
CONV_K = 31
PEER_HEADS = 8
PEER_KEYS = 128
PEER_EXPERTS = PEER_KEYS * PEER_KEYS
PEER_QDIM = 256
PEER_TOPK = 16
PEER_BLOCK = 128
IN_WIDTH = BRANCH_WIDTH + 6 * KV_WIDTH + 3 * N_HEADS + 2 * GMLP_WIDTH + 2 * CONV_WIDTH
NORM_EPS = 1e-6
NEG_BIG = -1e30
FORCE_SCORE = 1e9

kernel_name = 'hybrid_nsa_gmlp_conformer_peer_step'


def rms_norm(x, g):
    xf = x.astype(jnp.float32)
    y = xf * lax.rsqrt(jnp.mean(xf * xf, axis=-1, keepdims=True) + NORM_EPS)
    return (y * g.astype(jnp.float32)).astype(x.dtype)


def masked_softmax(s, mask):
    s = s.astype(jnp.float32)
    m = jnp.max(jnp.where(mask, s, NEG_BIG), axis=-1, keepdims=True)
    e = jnp.where(mask, jnp.exp(s - m), 0.0)
    return e / jnp.maximum(jnp.sum(e, axis=-1, keepdims=True), 1e-30)


def rope_partial(x, pos):
    half = ROT_DIM // 2
    freqs = ROPE_THETA ** (-jnp.arange(half, dtype=jnp.float32) / half)
    ang = pos.astype(jnp.float32)[:, None] * freqs[None, :]
    cos = jnp.cos(ang)[None, :, None, :]
    sin = jnp.sin(ang)[None, :, None, :]
    xr = x[..., :ROT_DIM].astype(jnp.float32)
    x1, x2 = xr[..., :half], xr[..., half:]
    rot = jnp.concatenate([x1 * cos - x2 * sin, x1 * sin + x2 * cos], axis=-1)
    return jnp.concatenate([rot.astype(x.dtype), x[..., ROT_DIM:]], axis=-1)


def adaln(c, w_ada, b_ada):
    mod = jax.nn.silu(c) @ w_ada + b_ada
    return jnp.split(mod, 6, axis=-1)


def modulate(x, g, shift, scale):
    return rms_norm(x, g) * (1.0 + scale[:, None, :]) + shift[:, None, :]


def mixer_front(x, shift, scale, norm_g, w_in, q_norm, k_norm, gmlp_norm, pos):
    b, t, _ = x.shape
    h = modulate(x, norm_g, shift, scale)
    z = h @ w_in
    sizes = [BRANCH_WIDTH, 6 * KV_WIDTH, 3 * N_HEADS, GMLP_WIDTH, GMLP_WIDTH, 2 * CONV_WIDTH]
    zq, zkv, zg, zu, zv, zglu = jnp.split(z, np.cumsum(sizes)[:-1].tolist(), axis=-1)
    q = rms_norm(zq.reshape(b, t, N_HEADS, HEAD_DIM), q_norm)
    q_n = q.reshape(b, t, KV_HEADS, Q_PER_KV, HEAD_DIM)
    q_r = rope_partial(q, pos).reshape(b, t, KV_HEADS, Q_PER_KV, HEAD_DIM)
    kv = zkv.reshape(b, t, 3, 2, KV_HEADS, HEAD_DIM)
    cmp_kv = kv[:, :, 0]
    ks = rope_partial(rms_norm(kv[:, :, 1, 0], k_norm[1]), pos)
    slc_kv = jnp.stack([ks, kv[:, :, 1, 1]], axis=2)
    kw = rope_partial(rms_norm(kv[:, :, 2, 0], k_norm[2]), pos)
    win_kv = jnp.stack([kw, kv[:, :, 2, 1]], axis=2)
    gates = jax.nn.sigmoid(zg.reshape(b, t, KV_HEADS, Q_PER_KV, 3))
    u = jax.nn.gelu(zu)
    v = rms_norm(jax.nn.gelu(zv), gmlp_norm)
    return h, q_r, q_n, gates, cmp_kv, slc_kv, win_kv, u, v, zglu


def compress(cmp_kv, pe, w1, b1, w2, kn):
    b, L = cmp_kv.shape[:2]
    n_seg = L // CMP_STRIDE
    nb = n_seg - CMP_RATIO + 1
    seg = cmp_kv[:, :n_seg * CMP_STRIDE].reshape(b, n_seg, CMP_STRIDE, 2, KV_HEADS, HEAD_DIM)
    w1r = w1.reshape(2, CMP_RATIO, CMP_STRIDE, HEAD_DIM, HEAD_DIM)
    acc = (jnp.einsum('cld,cldh->ch', pe, w1) + b1)[:, None, :]
    for r in range(CMP_RATIO):
        acc = acc + jnp.einsum('bnlcgd,cldh->bncgh', seg[:, r:r + nb], w1r[:, r])
    out = jnp.einsum('bncgh,chd->bncgd', jax.nn.gelu(acc), w2)
    kc = rms_norm(out[:, :, 0], kn)
    vc = out[:, :, 1]
    c_end = jnp.asarray(np.arange(nb) * CMP_STRIDE + CMP_BLOCK - 1, dtype=jnp.int32)
    return kc, vc, c_end


def slc_blocks(slc_kv):
    b, L = slc_kv.shape[:2]
    ns = -(-L // SLC_BLOCK)
    kv = jnp.pad(slc_kv, ((0, 0), (0, ns * SLC_BLOCK - L), (0, 0), (0, 0), (0, 0)))
    kv = kv.reshape(b, ns, SLC_BLOCK, 2, KV_HEADS, HEAD_DIM).transpose(3, 0, 4, 1, 2, 5)
    return kv[0], kv[1]


def overlap_matrix(nb, ns):
    cs = np.arange(nb) * CMP_STRIDE
    ce = cs + CMP_BLOCK - 1
    ss = np.arange(ns) * SLC_BLOCK
    se = ss + SLC_BLOCK - 1
    return jnp.asarray(((cs[:, None] <= se[None, :]) & (ce[:, None] >= ss[None, :])).astype(np.float32))


def nsa_attend(q_r, q_n, gates, pos_q, kc, vc, c_end, ovl, sk, sv, wk, wv, w_pos):
    b, tq = q_r.shape[:2]
    scale = HEAD_DIM ** -0.5
    s = jnp.einsum('bqgrd,bngd->bqgrn', q_n, kc) * scale
    p = masked_softmax(s, (c_end[None, :] <= pos_q[:, None])[None, :, None, None, :])
    o_c = jnp.einsum('bqgrn,bngd->bqgrd', p.astype(vc.dtype), vc)
    imp = jnp.einsum('bqgrn,ns->bgqs', p, ovl)
    ns = ovl.shape[1]
    j = jnp.arange(ns, dtype=jnp.int32)[None, :]
    cur = (pos_q // SLC_BLOCK)[:, None]
    valid = j <= cur
    forced = (j == 0) | (j == cur) | (j == cur - 1)
    imp = jnp.where(valid[None, None], jnp.where(forced[None, None], FORCE_SCORE, imp), -FORCE_SCORE)
    n_sel = min(N_SELECT, ns)
    sel = lax.top_k(imp, n_sel)[1]
    gather = jax.vmap(jax.vmap(lambda kb, si: kb[si]))
    ksel = gather(sk, sel)
    vsel = gather(sv, sel)
    tok = sel[..., None] * SLC_BLOCK + jnp.arange(SLC_BLOCK, dtype=jnp.int32)
    ms = (tok <= pos_q[None, None, :, None, None]).transpose(0, 2, 1, 3, 4).reshape(b, tq, KV_HEADS, 1, n_sel * SLC_BLOCK)
    s2 = jnp.einsum('bqgrd,bgqnld->bqgrnl', q_r, ksel).reshape(b, tq, KV_HEADS, Q_PER_KV, n_sel * SLC_BLOCK) * scale
    p2 = masked_softmax(s2, ms)
    o_s = jnp.einsum('bqgrk,bgqkd->bqgrd', p2.astype(vsel.dtype), vsel.reshape(b, KV_HEADS, tq, n_sel * SLC_BLOCK, HEAD_DIM))
    s3 = jnp.einsum('bqgrd,bkgd->bqgrk', q_r, wk) * scale
    dlt = pos_q[:, None] - w_pos[None, :]
    mw = (dlt >= 0) & (dlt < WINDOW) & (w_pos[None, :] >= 0)
    p3 = masked_softmax(s3, mw[None, :, None, None, :])
    o_w = jnp.einsum('bqgrk,bkgd->bqgrd', p3.astype(wv.dtype), wv)
    return gates[..., 0:1] * o_c + gates[..., 1:2] * o_s + gates[..., 2:3] * o_w


def nsa_prompt(q_r, q_n, gates, kc, vc, c_end, ovl, sk, sv, win_kv):
    b, t = q_r.shape[:2]
    wpad = jnp.pad(win_kv, ((0, 0), (WINDOW, 0), (0, 0), (0, 0), (0, 0)))
    span = WINDOW + Q_BLOCK

    def block(i):
        q0 = i * Q_BLOCK
        sl = lambda a: lax.dynamic_slice_in_dim(a, q0, Q_BLOCK, axis=1)
        wkv = lax.dynamic_slice_in_dim(wpad, q0, span, axis=1)
        pos_q = q0 + jnp.arange(Q_BLOCK, dtype=jnp.int32)
        w_pos = q0 - WINDOW + jnp.arange(span, dtype=jnp.int32)
        return nsa_attend(sl(q_r), sl(q_n), sl(gates), pos_q, kc, vc, c_end, ovl, sk, sv,
                          wkv[:, :, 0], wkv[:, :, 1], w_pos)

    o = lax.map(block, jnp.arange(t // Q_BLOCK, dtype=jnp.int32))
    return jnp.moveaxis(o, 0, 1).reshape(b, t, BRANCH_WIDTH)


def gmlp_mix(u, v, w_s, b_s):
    b, t, _ = v.shape
    nc = -(-t // CHUNK)
    vp = jnp.pad(v, ((0, 0), (0, nc * CHUNK - t), (0, 0))).reshape(b, nc, CHUNK, GMLP_GROUPS, GMLP_WIDTH // GMLP_GROUPS)
    w = w_s * jnp.tril(jnp.ones((CHUNK, CHUNK), w_s.dtype))
    mixed = jnp.einsum('gpq,bcqgd->bcpgd', w, vp) + b_s.T[None, None, :, :, None]
    return u * mixed.reshape(b, nc * CHUNK, GMLP_WIDTH)[:, :t]


def conformer_conv(zglu, buf, w_dw, b_dw, norm_g):
    a, gt = jnp.split(zglu, 2, axis=-1)
    glu = a * jax.nn.sigmoid(gt)
    xin = jnp.concatenate([buf.astype(glu.dtype), glu], axis=1)
    y = lax.conv_general_dilated(xin, w_dw[:, None, :].astype(glu.dtype), (1,), 'VALID',
                                 dimension_numbers=('NWC', 'WIO', 'NWC'), feature_group_count=CONV_WIDTH) + b_dw
    y = jax.nn.silu(rms_norm(y, norm_g))
    return y, xin[:, xin.shape[1] - (CONV_K - 1):]


def mixer_merge(x, h, o_nsa, o_gmlp, o_conv, gate, w_branch, w_merge, b_merge, w_out):
    b, t, d = x.shape
    mg = jax.nn.sigmoid(h @ w_merge + b_merge).reshape(b, t, 3, d)
    y = (mg[:, :, 0] * (o_nsa @ w_branch[0]) + mg[:, :, 1] * (o_gmlp @ w_branch[1])
         + mg[:, :, 2] * (o_conv @ w_branch[2]))
    return x + gate[:, None, :] * (y @ w_out)


def peer_ffn(h, w_pq, sub_keys, u_tab, v_tab):
    b, t, d = h.shape
    n_tok = b * t
    nblk = -(-n_tok // PEER_BLOCK)
    xt = jnp.pad(h.reshape(n_tok, d), ((0, nblk * PEER_BLOCK - n_tok), (0, 0))).reshape(nblk, PEER_BLOCK, d)

    def block(xb):
        q = (xb @ w_pq).reshape(PEER_BLOCK, PEER_HEADS, 2, PEER_QDIM // 2)
        s = jnp.einsum('thcd,hckd->thck', q, sub_keys)
        sv, si = lax.top_k(s, PEER_TOPK)
        cand = (sv[:, :, 0, :, None] + sv[:, :, 1, None, :]).reshape(PEER_BLOCK, PEER_HEADS, PEER_TOPK * PEER_TOPK)
        cv, ci = lax.top_k(cand, PEER_TOPK)
        e = (jnp.take_along_axis(si[:, :, 0], ci // PEER_TOPK, axis=-1) * PEER_KEYS
             + jnp.take_along_axis(si[:, :, 1], ci % PEER_TOPK, axis=-1)).reshape(PEER_BLOCK, PEER_HEADS * PEER_TOPK)
        g = jax.nn.softmax(cv.astype(jnp.float32), axis=-1).reshape(PEER_BLOCK, PEER_HEADS * PEER_TOPK)
        act = jax.nn.gelu(jnp.einsum('tkd,td->tk', u_tab[e], xb))
        return jnp.einsum('tk,tkd->td', g.astype(xb.dtype) * act, v_tab[e])

    out = lax.map(block, xt).reshape(nblk * PEER_BLOCK, d)[:n_tok]
    return out.reshape(b, t, d)


def setup_inputs(seed: int = 0) -> dict:
    key = jax.random.key(seed)
    ks = iter(jax.random.split(key, 48))
    nrm = lambda shape, scale: jax.random.normal(next(ks), shape, jnp.float32) * scale
    D = D_MODEL
    n_pages = PAST_LEN // PAGE_SIZE
    n_phys = (DEC_BATCH * n_pages * 5) // 4
    wb = min(WINDOW, PAST_LEN)
    perm = jax.random.permutation(next(ks), n_phys)
    page_table = perm[:DEC_BATCH * n_pages].reshape(DEC_BATCH, n_pages).astype(jnp.int32)
    return {
        'x_prompt': nrm((BATCH, SEQ, D), 1.0),
        'x_sample': nrm((DEC_BATCH, DEC_SEQ, D), 1.0),
        'cache_cmp_kv': nrm((DEPTH, n_phys, PAGE_SIZE, 2, KV_HEADS, HEAD_DIM), 1.0),
        'cache_slc_kv': nrm((DEPTH, n_phys, PAGE_SIZE, 2, KV_HEADS, HEAD_DIM), 1.0),
        'state_win_kv': nrm((DEPTH, DEC_BATCH, wb, 2, KV_HEADS, HEAD_DIM), 1.0),
        'state_conv': nrm((DEPTH, DEC_BATCH, CONV_K - 1, CONV_WIDTH), 0.5),
        'page_table': page_table,
        'c_prompt': nrm((BATCH, D), 1.0),
        'c_sample': nrm((DEC_BATCH, D), 1.0),
        'w_ada': nrm((DEPTH, D, 6 * D), 0.5 * D ** -0.5),
        'b_ada': nrm((DEPTH, 6 * D), 0.02),
        'norm_mix': 1.0 + nrm((DEPTH, D), 0.02),
        'norm_ffn': 1.0 + nrm((DEPTH, D), 0.02),
        'w_in': nrm((DEPTH, D, IN_WIDTH), D ** -0.5),
        'q_norm': 1.0 + nrm((DEPTH, HEAD_DIM), 0.02),
        'k_norm': 1.0 + nrm((DEPTH, 3, HEAD_DIM), 0.02),
        'cmp_pe': nrm((DEPTH, 2, CMP_BLOCK, HEAD_DIM), 0.02),
        'cmp_w1': nrm((DEPTH, 2, CMP_BLOCK, HEAD_DIM, HEAD_DIM), (CMP_BLOCK * HEAD_DIM) ** -0.5),
        'cmp_b1': nrm((DEPTH, 2, HEAD_DIM), 0.02),
        'cmp_w2': nrm((DEPTH, 2, HEAD_DIM, HEAD_DIM), HEAD_DIM ** -0.5),
        'gmlp_norm': 1.0 + nrm((DEPTH, GMLP_WIDTH), 0.02),
        'gmlp_ws': nrm((DEPTH, GMLP_GROUPS, CHUNK, CHUNK), CHUNK ** -0.5),
        'gmlp_bs': 1.0 + nrm((DEPTH, GMLP_GROUPS, CHUNK), 0.02),
        'conv_w': nrm((DEPTH, CONV_K, CONV_WIDTH), CONV_K ** -0.5),
        'conv_b': nrm((DEPTH, CONV_WIDTH), 0.02),
        'conv_norm': 1.0 + nrm((DEPTH, CONV_WIDTH), 0.02),
        'w_branch': nrm((DEPTH, 3, BRANCH_WIDTH, D), BRANCH_WIDTH ** -0.5),
        'w_merge': nrm((DEPTH, D, 3 * D), D ** -0.5),
        'b_merge': nrm((DEPTH, 3 * D), 0.02),
        'w_out': nrm((DEPTH, D, D), D ** -0.5),
        'peer_wq': nrm((DEPTH, D, PEER_HEADS * PEER_QDIM), D ** -0.5),
        'peer_keys': nrm((DEPTH, PEER_HEADS, 2, PEER_KEYS, PEER_QDIM // 2), (PEER_QDIM // 2) ** -0.5),
        'peer_u': nrm((DEPTH, PEER_EXPERTS, D), D ** -0.5),
        'peer_v': nrm((DEPTH, PEER_EXPERTS, D), 0.5),
    }


def reference(x_prompt, x_sample, cache_cmp_kv, cache_slc_kv, state_win_kv, state_conv, page_table,
              c_prompt, c_sample, w_ada, b_ada, norm_mix, norm_ffn, w_in, q_norm, k_norm,
              cmp_pe, cmp_w1, cmp_b1, cmp_w2, gmlp_norm, gmlp_ws, gmlp_bs, conv_w, conv_b, conv_norm,
              w_branch, w_merge, b_merge, w_out, peer_wq, peer_keys, peer_u, peer_v):
    bp, t = x_prompt.shape[:2]
    bs, ds = x_sample.shape[:2]
    past_len = page_table.shape[1] * PAGE_SIZE
    pos_p = jnp.arange(t, dtype=jnp.int32)
    pos_s = past_len + jnp.arange(ds, dtype=jnp.int32)
    xp, xs = x_prompt, x_sample
    cmp_p, cmp_s, slc_p, slc_s, win_p, win_s, conv_p, conv_s, gv_s = [], [], [], [], [], [], [], [], []
    for l in range(DEPTH):
        mp = adaln(c_prompt, w_ada[l], b_ada[l])
        msm = adaln(c_sample, w_ada[l], b_ada[l])
        hp, q_r, q_n, gates, ckv, skv, wkv, u, v, zglu = mixer_front(
            xp, mp[0], mp[1], norm_mix[l], w_in[l], q_norm[l], k_norm[l], gmlp_norm[l], pos_p)
        kc, vc, c_end = compress(ckv, cmp_pe[l], cmp_w1[l], cmp_b1[l], cmp_w2[l], k_norm[l, 0])
        sk, sv = slc_blocks(skv)
        ovl = overlap_matrix(kc.shape[1], sk.shape[2])
        o_nsa = nsa_prompt(q_r, q_n, gates, kc, vc, c_end, ovl, sk, sv, wkv)
        o_g = gmlp_mix(u, v, gmlp_ws[l], gmlp_bs[l])
        o_c, buf = conformer_conv(zglu, jnp.zeros((bp, CONV_K - 1, CONV_WIDTH), zglu.dtype),
                                  conv_w[l], conv_b[l], conv_norm[l])
        xp = mixer_merge(xp, hp, o_nsa, o_g, o_c, mp[2], w_branch[l], w_merge[l], b_merge[l], w_out[l])
        xp = xp + mp[5][:, None, :] * peer_ffn(modulate(xp, norm_ffn[l], mp[3], mp[4]),
                                               peer_wq[l], peer_keys[l], peer_u[l], peer_v[l])
        cmp_p.append(ckv)
        slc_p.append(skv)
        win_p.append(wkv[:, t - min(WINDOW, t):])
        conv_p.append(buf)
        hs, q_r, q_n, gates, ckv, skv, wkv, u, v, zglu = mixer_front(
            xs, msm[0], msm[1], norm_mix[l], w_in[l], q_norm[l], k_norm[l], gmlp_norm[l], pos_s)
        past_c = cache_cmp_kv[l][page_table].reshape(bs, past_len, 2, KV_HEADS, HEAD_DIM)
        past_s = cache_slc_kv[l][page_table].reshape(bs, past_len, 2, KV_HEADS, HEAD_DIM)
        kc, vc, c_end = compress(jnp.concatenate([past_c, ckv], axis=1),
                                 cmp_pe[l], cmp_w1[l], cmp_b1[l], cmp_w2[l], k_norm[l, 0])
        sk, sv = slc_blocks(jnp.concatenate([past_s, skv], axis=1))
        ovl = overlap_matrix(kc.shape[1], sk.shape[2])
        wb = state_win_kv.shape[2]
        win_full = jnp.concatenate([state_win_kv[l], wkv], axis=1)
        w_pos = past_len - wb + jnp.arange(wb + ds, dtype=jnp.int32)
        o_nsa = nsa_attend(q_r, q_n, gates, pos_s, kc, vc, c_end, ovl, sk, sv,
                           win_full[:, :, 0], win_full[:, :, 1], w_pos).reshape(bs, ds, BRANCH_WIDTH)
        o_g = gmlp_mix(u, v, gmlp_ws[l], gmlp_bs[l])
        o_c, buf = conformer_conv(zglu, state_conv[l], conv_w[l], conv_b[l], conv_norm[l])
        xs = mixer_merge(xs, hs, o_nsa, o_g, o_c, msm[2], w_branch[l], w_merge[l], b_merge[l], w_out[l])
        xs = xs + msm[5][:, None, :] * peer_ffn(modulate(xs, norm_ffn[l], msm[3], msm[4]),
                                                peer_wq[l], peer_keys[l], peer_u[l], peer_v[l])
        cmp_s.append(ckv)
        slc_s.append(skv)
        lw = win_full.shape[1]
        win_s.append(win_full[:, lw - min(WINDOW, lw):])
        conv_s.append(buf)
        gv_s.append(v)
    return (xp, xs, jnp.stack(cmp_p), jnp.stack(cmp_s), jnp.stack(slc_p), jnp.stack(slc_s),
            jnp.stack(win_p), jnp.stack(win_s), jnp.stack(conv_p), jnp.stack(conv_s), jnp.stack(gv_s))
```

```python
import functools

import jax
import jax.numpy as jnp
import numpy as np
from jax import lax
from jax.experimental import pallas as pl
from jax.experimental.pallas import tpu as pltpu

D_MODEL = 2048
PAGE_SIZE = 128
BRANCH_WIDTH = D_MODEL // 2
HEAD_DIM = 64
N_HEADS = BRANCH_WIDTH // HEAD_DIM
KV_HEADS = 4
Q_PER_KV = N_HEADS // KV_HEADS
KV_WIDTH = KV_HEADS * HEAD_DIM
ROT_DIM = HEAD_DIM // 4
ROPE_THETA = 500000.0
CMP_BLOCK = 32
CMP_STRIDE = 16
CMP_RATIO = CMP_BLOCK // CMP_STRIDE
SLC_BLOCK = 64
N_SELECT = 16
WINDOW = 512
Q_BLOCK = 128
GMLP_WIDTH = BRANCH_WIDTH
GMLP_GROUPS = 8
CHUNK = 128
CONV_WIDTH = BRANCH_WIDTH
CONV_K = 31
PEER_HEADS = 8
PEER_KEYS = 128
PEER_QDIM = 256
PEER_TOPK = 16
PEER_BLOCK = 128
NORM_EPS = 1e-6
NEG_BIG = -1e30
FORCE_SCORE = 1e9

VMEM_LIMIT_BYTES = 56 * 1024 * 1024


def _mm_kernel(a_ref, b_ref, o_ref):
    a = a_ref[...].astype(jnp.bfloat16)
    b = b_ref[...].astype(jnp.bfloat16)
    o_ref[...] = jnp.dot(a, b, preferred_element_type=jnp.float32)


def _pick_tm(m):
    for tm in (512, 256, 128):
        if m % tm == 0:
            return tm
    return m


def mm(a, b, tn=512):
    m, k = a.shape
    _, n = b.shape
    tm = _pick_tm(m)
    tn = min(tn, n)
    return pl.pallas_call(
        _mm_kernel,
        out_shape=jax.ShapeDtypeStruct((m, n), jnp.float32),
        grid=(m // tm, pl.cdiv(n, tn)),
        in_specs=[pl.BlockSpec((tm, k), lambda i, j: (i, 0)),
                  pl.BlockSpec((k, tn), lambda i, j: (0, j))],
        out_specs=pl.BlockSpec((tm, tn), lambda i, j: (i, j)),
        compiler_params=pltpu.CompilerParams(
            dimension_semantics=("parallel", "arbitrary"),
            vmem_limit_bytes=VMEM_LIMIT_BYTES),
        name="mm",
    )(a, b)


def mm3(x, w):
    b, t, k = x.shape
    return mm(x.reshape(b * t, k), w).reshape(b, t, w.shape[1])


def rms_norm(x, g):
    xf = x.astype(jnp.float32)
    y = xf * lax.rsqrt(jnp.mean(xf * xf, axis=-1, keepdims=True) + NORM_EPS)
    return (y * g.astype(jnp.float32)).astype(x.dtype)


def masked_softmax(s, mask):
    s = s.astype(jnp.float32)
    m = jnp.max(jnp.where(mask, s, NEG_BIG), axis=-1, keepdims=True)
    e = jnp.where(mask, jnp.exp(s - m), 0.0)
    return e / jnp.maximum(jnp.sum(e, axis=-1, keepdims=True), 1e-30)


def rope_partial(x, pos):
    half = ROT_DIM // 2
    freqs = ROPE_THETA ** (-jnp.arange(half, dtype=jnp.float32) / half)
    ang = pos.astype(jnp.float32)[:, None] * freqs[None, :]
    cos = jnp.cos(ang)[None, :, None, :]
    sin = jnp.sin(ang)[None, :, None, :]
    xr = x[..., :ROT_DIM].astype(jnp.float32)
    x1, x2 = xr[..., :half], xr[..., half:]
    rot = jnp.concatenate([x1 * cos - x2 * sin, x1 * sin + x2 * cos], axis=-1)
    return jnp.concatenate([rot.astype(x.dtype), x[..., ROT_DIM:]], axis=-1)


def adaln(c_prompt, c_sample, w_ada, b_ada):
    bp, bs = c_prompt.shape[0], c_sample.shape[0]
    rows = bp + bs
    pad = (-rows) % 8
    c_all = jnp.concatenate([c_prompt, c_sample, jnp.zeros((pad, c_prompt.shape[1]), c_prompt.dtype)], axis=0)
    mod = mm(jax.nn.silu(c_all), w_ada) + b_ada
    return jnp.split(mod[:bp], 6, axis=-1), jnp.split(mod[bp:rows], 6, axis=-1)


def modulate(x, g, shift, scale):
    return rms_norm(x, g) * (1.0 + scale[:, None, :]) + shift[:, None, :]


def mixer_front(x, shift, scale, norm_g, w_in, q_norm, k_norm, gmlp_norm, pos):
    b, t, _ = x.shape
    h = modulate(x, norm_g, shift, scale)
    z = mm3(h, w_in)
    sizes = [BRANCH_WIDTH, 6 * KV_WIDTH, 3 * N_HEADS, GMLP_WIDTH, GMLP_WIDTH, 2 * CONV_WIDTH]
    zq, zkv, zg, zu, zv, zglu = jnp.split(z, np.cumsum(sizes)[:-1].tolist(), axis=-1)
    q = rms_norm(zq.reshape(b, t, N_HEADS, HEAD_DIM), q_norm)
    q_n = q.reshape(b, t, KV_HEADS, Q_PER_KV, HEAD_DIM)
    q_r = rope_partial(q, pos).reshape(b, t, KV_HEADS, Q_PER_KV, HEAD_DIM)
    kv = zkv.reshape(b, t, 3, 2, KV_HEADS, HEAD_DIM)
    cmp_kv = kv[:, :, 0]
    ks = rope_partial(rms_norm(kv[:, :, 1, 0], k_norm[1]), pos)
    slc_kv = jnp.stack([ks, kv[:, :, 1, 1]], axis=2)
    kw = rope_partial(rms_norm(kv[:, :, 2, 0], k_norm[2]), pos)
    win_kv = jnp.stack([kw, kv[:, :, 2, 1]], axis=2)
    gates = jax.nn.sigmoid(zg.reshape(b, t, KV_HEADS, Q_PER_KV, 3))
    u = jax.nn.gelu(zu)
    v = rms_norm(jax.nn.gelu(zv), gmlp_norm)
    return h, q_r, q_n, gates, cmp_kv, slc_kv, win_kv, u, v, zglu


def compress(cmp_kv, pe, w1, b1, w2, kn):
    b, L = cmp_kv.shape[:2]
    n_seg = L // CMP_STRIDE
    nb = n_seg - CMP_RATIO + 1
    seg = cmp_kv[:, :n_seg * CMP_STRIDE].reshape(b, n_seg, CMP_STRIDE, 2, KV_HEADS, HEAD_DIM)
    w1r = w1.reshape(2, CMP_RATIO, CMP_STRIDE, HEAD_DIM, HEAD_DIM)
    acc = (jnp.einsum('cld,cldh->ch', pe, w1) + b1)[:, None, :]
    for r in range(CMP_RATIO):
        acc = acc + jnp.einsum('bnlcgd,cldh->bncgh', seg[:, r:r + nb], w1r[:, r])
    out = jnp.einsum('bncgh,chd->bncgd', jax.nn.gelu(acc), w2)
    kc = rms_norm(out[:, :, 0], kn)
    vc = out[:, :, 1]
    c_end = jnp.asarray(np.arange(nb) * CMP_STRIDE + CMP_BLOCK - 1, dtype=jnp.int32)
    return kc, vc, c_end


def slc_blocks(slc_kv):
    b, L = slc_kv.shape[:2]
    ns = -(-L // SLC_BLOCK)
    kv = jnp.pad(slc_kv, ((0, 0), (0, ns * SLC_BLOCK - L), (0, 0), (0, 0), (0, 0)))
    kv = kv.reshape(b, ns, SLC_BLOCK, 2, KV_HEADS, HEAD_DIM).transpose(3, 0, 4, 1, 2, 5)
    return kv[0], kv[1]


def overlap_matrix(nb, ns):
    cs = np.arange(nb) * CMP_STRIDE
    ce = cs + CMP_BLOCK - 1
    ss = np.arange(ns) * SLC_BLOCK
    se = ss + SLC_BLOCK - 1
    return jnp.asarray(((cs[:, None] <= se[None, :]) & (ce[:, None] >= ss[None, :])).astype(np.float32))


def nsa_attend(q_r, q_n, gates, pos_q, kc, vc, c_end, ovl, sk, sv, wk, wv, w_pos):
    b, tq = q_r.shape[:2]
    scale = HEAD_DIM ** -0.5
    s = jnp.einsum('bqgrd,bngd->bqgrn', q_n, kc) * scale
    p = masked_softmax(s, (c_end[None, :] <= pos_q[:, None])[None, :, None, None, :])
    o_c = jnp.einsum('bqgrn,bngd->bqgrd', p.astype(vc.dtype), vc)
    imp = jnp.einsum('bqgrn,ns->bgqs', p, ovl)
    ns = ovl.shape[1]
    j = jnp.arange(ns, dtype=jnp.int32)[None, :]
    cur = (pos_q // SLC_BLOCK)[:, None]
    valid = j <= cur
    forced = (j == 0) | (j == cur) | (j == cur - 1)
    imp = jnp.where(valid[None, None], jnp.where(forced[None, None], FORCE_SCORE, imp), -FORCE_SCORE)
    n_sel = min(N_SELECT, ns)
    sel = lax.top_k(imp, n_sel)[1]
    gather = jax.vmap(jax.vmap(lambda kb, si: kb[si]))
    ksel = gather(sk, sel)
    vsel = gather(sv, sel)
    tok = sel[..., None] * SLC_BLOCK + jnp.arange(SLC_BLOCK, dtype=jnp.int32)
    ms = (tok <= pos_q[None, None, :, None, None]).transpose(0, 2, 1, 3, 4).reshape(b, tq, KV_HEADS, 1, n_sel * SLC_BLOCK)
    s2 = jnp.einsum('bqgrd,bgqnld->bqgrnl', q_r, ksel).reshape(b, tq, KV_HEADS, Q_PER_KV, n_sel * SLC_BLOCK) * scale
    p2 = masked_softmax(s2, ms)
    o_s = jnp.einsum('bqgrk,bgqkd->bqgrd', p2.astype(vsel.dtype), vsel.reshape(b, KV_HEADS, tq, n_sel * SLC_BLOCK, HEAD_DIM))
    s3 = jnp.einsum('bqgrd,bkgd->bqgrk', q_r, wk) * scale
    dlt = pos_q[:, None] - w_pos[None, :]
    mw = (dlt >= 0) & (dlt < WINDOW) & (w_pos[None, :] >= 0)
    p3 = masked_softmax(s3, mw[None, :, None, None, :])
    o_w = jnp.einsum('bqgrk,bkgd->bqgrd', p3.astype(wv.dtype), wv)
    return gates[..., 0:1] * o_c + gates[..., 1:2] * o_s + gates[..., 2:3] * o_w


def nsa_prompt(q_r, q_n, gates, kc, vc, c_end, ovl, sk, sv, win_kv):
    b, t = q_r.shape[:2]
    wpad = jnp.pad(win_kv, ((0, 0), (WINDOW, 0), (0, 0), (0, 0), (0, 0)))
    span = WINDOW + Q_BLOCK

    def block(i):
        q0 = i * Q_BLOCK
        sl = lambda a: lax.dynamic_slice_in_dim(a, q0, Q_BLOCK, axis=1)
        wkv = lax.dynamic_slice_in_dim(wpad, q0, span, axis=1)
        pos_q = q0 + jnp.arange(Q_BLOCK, dtype=jnp.int32)
        w_pos = q0 - WINDOW + jnp.arange(span, dtype=jnp.int32)
        return nsa_attend(sl(q_r), sl(q_n), sl(gates), pos_q, kc, vc, c_end, ovl, sk, sv,
                          wkv[:, :, 0], wkv[:, :, 1], w_pos)

    o = lax.map(block, jnp.arange(t // Q_BLOCK, dtype=jnp.int32))
    return jnp.moveaxis(o, 0, 1).reshape(b, t, BRANCH_WIDTH)


def gmlp_mix(u, v, w_s, b_s):
    b, t, _ = v.shape
    nc = -(-t // CHUNK)
    vp = jnp.pad(v, ((0, 0), (0, nc * CHUNK - t), (0, 0))).reshape(b, nc, CHUNK, GMLP_GROUPS, GMLP_WIDTH // GMLP_GROUPS)
    w = w_s * jnp.tril(jnp.ones((CHUNK, CHUNK), w_s.dtype))
    mixed = jnp.einsum('gpq,bcqgd->bcpgd', w, vp) + b_s.T[None, None, :, :, None]
    return u * mixed.reshape(b, nc * CHUNK, GMLP_WIDTH)[:, :t]


def conformer_conv(zglu, buf, w_dw, b_dw, norm_g):
    a, gt = jnp.split(zglu, 2, axis=-1)
    glu = a * jax.nn.sigmoid(gt)
    xin = jnp.concatenate([buf.astype(glu.dtype), glu], axis=1)
    y = lax.conv_general_dilated(xin, w_dw[:, None, :].astype(glu.dtype), (1,), 'VALID',
                                 dimension_numbers=('NWC', 'WIO', 'NWC'), feature_group_count=CONV_WIDTH) + b_dw
    y = jax.nn.silu(rms_norm(y, norm_g))
    return y, xin[:, xin.shape[1] - (CONV_K - 1):]


def mixer_merge(x, h, o_nsa, o_gmlp, o_conv, gate, w_branch, w_merge, b_merge, w_out):
    b, t, d = x.shape
    mg = jax.nn.sigmoid(mm3(h, w_merge) + b_merge).reshape(b, t, 3, d)
    y = (mg[:, :, 0] * mm3(o_nsa, w_branch[0]) + mg[:, :, 1] * mm3(o_gmlp, w_branch[1])
         + mg[:, :, 2] * mm3(o_conv, w_branch[2]))
    return x + gate[:, None, :] * mm3(y, w_out)


def peer_ffn(h, w_pq, sub_keys, u_tab, v_tab):
    b, t, d = h.shape
    n_tok = b * t
    nblk = -(-n_tok // PEER_BLOCK)
    hq = mm(h.reshape(n_tok, d), w_pq)
    xt = jnp.pad(h.reshape(n_tok, d), ((0, nblk * PEER_BLOCK - n_tok), (0, 0))).reshape(nblk, PEER_BLOCK, d)
    qt = jnp.pad(hq, ((0, nblk * PEER_BLOCK - n_tok), (0, 0))).reshape(nblk, PEER_BLOCK, -1)

    def block(args):
        xb, qb = args
        q = qb.reshape(PEER_BLOCK, PEER_HEADS, 2, PEER_QDIM // 2)
        s = jnp.einsum('thcd,hckd->thck', q, sub_keys)
        sv, si = lax.top_k(s, PEER_TOPK)
        cand = (sv[:, :, 0, :, None] + sv[:, :, 1, None, :]).reshape(PEER_BLOCK, PEER_HEADS, PEER_TOPK * PEER_TOPK)
        cv, ci = lax.top_k(cand, PEER_TOPK)
        e = (jnp.take_along_axis(si[:, :, 0], ci // PEER_TOPK, axis=-1) * PEER_KEYS
             + jnp.take_along_axis(si[:, :, 1], ci % PEER_TOPK, axis=-1)).reshape(PEER_BLOCK, PEER_HEADS * PEER_TOPK)
        g = jax.nn.softmax(cv.astype(jnp.float32), axis=-1).reshape(PEER_BLOCK, PEER_HEADS * PEER_TOPK)
        act = jax.nn.gelu(jnp.einsum('tkd,td->tk', u_tab[e], xb))
        return jnp.einsum('tk,tkd->td', g.astype(xb.dtype) * act, v_tab[e])

    out = lax.map(block, (xt, qt)).reshape(nblk * PEER_BLOCK, d)[:n_tok]
    return out.reshape(b, t, d)


def kernel(x_prompt, x_sample, cache_cmp_kv, cache_slc_kv, state_win_kv, state_conv, page_table,
           c_prompt, c_sample, w_ada, b_ada, norm_mix, norm_ffn, w_in, q_norm, k_norm,
           cmp_pe, cmp_w1, cmp_b1, cmp_w2, gmlp_norm, gmlp_ws, gmlp_bs, conv_w, conv_b, conv_norm,
           w_branch, w_merge, b_merge, w_out, peer_wq, peer_keys, peer_u, peer_v):
    bp, t = x_prompt.shape[:2]
    bs, ds = x_sample.shape[:2]
    depth = w_in.shape[0]
    past_len = page_table.shape[1] * PAGE_SIZE
    pos_p = jnp.arange(t, dtype=jnp.int32)
    pos_s = past_len + jnp.arange(ds, dtype=jnp.int32)
    xp, xs = x_prompt, x_sample
    cmp_p, cmp_s, slc_p, slc_s, win_p, win_s, conv_p, conv_s, gv_s = [], [], [], [], [], [], [], [], []
    for l in range(depth):
        mp, msm = adaln(c_prompt, c_sample, w_ada[l], b_ada[l])
        hp, q_r, q_n, gates, ckv, skv, wkv, u, v, zglu = mixer_front(
            xp, mp[0], mp[1], norm_mix[l], w_in[l], q_norm[l], k_norm[l], gmlp_norm[l], pos_p)
        kc, vc, c_end = compress(ckv, cmp_pe[l], cmp_w1[l], cmp_b1[l], cmp_w2[l], k_norm[l, 0])
        sk, sv = slc_blocks(skv)
        ovl = overlap_matrix(kc.shape[1], sk.shape[2])
        o_nsa = nsa_prompt(q_r, q_n, gates, kc, vc, c_end, ovl, sk, sv, wkv)
        o_g = gmlp_mix(u, v, gmlp_ws[l], gmlp_bs[l])
        o_c, buf = conformer_conv(zglu, jnp.zeros((bp, CONV_K - 1, CONV_WIDTH), zglu.dtype),
                                  conv_w[l], conv_b[l], conv_norm[l])
        xp = mixer_merge(xp, hp, o_nsa, o_g, o_c, mp[2], w_branch[l], w_merge[l], b_merge[l], w_out[l])
        xp = xp + mp[5][:, None, :] * peer_ffn(modulate(xp, norm_ffn[l], mp[3], mp[4]),
                                               peer_wq[l], peer_keys[l], peer_u[l], peer_v[l])
        cmp_p.append(ckv)
        slc_p.append(skv)
        win_p.append(wkv[:, t - min(WINDOW, t):])
        conv_p.append(buf)
        hs, q_r, q_n, gates, ckv, skv, wkv, u, v, zglu = mixer_front(
            xs, msm[0], msm[1], norm_mix[l], w_in[l], q_norm[l], k_norm[l], gmlp_norm[l], pos_s)
        past_c = cache_cmp_kv[l][page_table].reshape(bs, past_len, 2, KV_HEADS, HEAD_DIM)
        past_s = cache_slc_kv[l][page_table].reshape(bs, past_len, 2, KV_HEADS, HEAD_DIM)
        kc, vc, c_end = compress(jnp.concatenate([past_c, ckv], axis=1),
                                 cmp_pe[l], cmp_w1[l], cmp_b1[l], cmp_w2[l], k_norm[l, 0])
        sk, sv = slc_blocks(jnp.concatenate([past_s, skv], axis=1))
        ovl = overlap_matrix(kc.shape[1], sk.shape[2])
        wb = state_win_kv.shape[2]
        win_full = jnp.concatenate([state_win_kv[l], wkv], axis=1)
        w_pos = past_len - wb + jnp.arange(wb + ds, dtype=jnp.int32)
        o_nsa = nsa_attend(q_r, q_n, gates, pos_s, kc, vc, c_end, ovl, sk, sv,
                           win_full[:, :, 0], win_full[:, :, 1], w_pos).reshape(bs, ds, BRANCH_WIDTH)
        o_g = gmlp_mix(u, v, gmlp_ws[l], gmlp_bs[l])
        o_c, buf = conformer_conv(zglu, state_conv[l], conv_w[l], conv_b[l], conv_norm[l])
        xs = mixer_merge(xs, hs, o_nsa, o_g, o_c, msm[2], w_branch[l], w_merge[l], b_merge[l], w_out[l])
        xs = xs + msm[5][:, None, :] * peer_ffn(modulate(xs, norm_ffn[l], msm[3], msm[4]),
                                                peer_wq[l], peer_keys[l], peer_u[l], peer_v[l])
        cmp_s.append(ckv)
        slc_s.append(skv)
        lw = win_full.shape[1]
        win_s.append(win_full[:, lw - min(WINDOW, lw):])
        conv_s.append(buf)
        gv_s.append(v)
    return (xp, xs, jnp.stack(cmp_p), jnp.stack(cmp_s), jnp.stack(slc_p), jnp.stack(slc_s),
            jnp.stack(win_p), jnp.stack(win_s), jnp.stack(conv_p), jnp.stack(conv_s), jnp.stack(gv_s))
```

```python
import functools

import jax
import jax.numpy as jnp
import numpy as np
from jax import lax
from jax.experimental import pallas as pl
from jax.experimental.pallas import tpu as pltpu

D_MODEL = 2048
PAGE_SIZE = 128
BRANCH_WIDTH = D_MODEL // 2
HEAD_DIM = 64
N_HEADS = BRANCH_WIDTH // HEAD_DIM
KV_HEADS = 4
Q_PER_KV = N_HEADS // KV_HEADS
KV_WIDTH = KV_HEADS * HEAD_DIM
ROT_DIM = HEAD_DIM // 4
ROPE_THETA = 500000.0
CMP_BLOCK = 32
CMP_STRIDE = 16
CMP_RATIO = CMP_BLOCK // CMP_STRIDE
SLC_BLOCK = 64
N_SELECT = 16
WINDOW = 512
Q_BLOCK = 128
GMLP_WIDTH = BRANCH_WIDTH
GMLP_GROUPS = 8
CHUNK = 128
CONV_WIDTH = BRANCH_WIDTH
CONV_K = 31
PEER_HEADS = 8
PEER_KEYS = 128
PEER_QDIM = 256
PEER_TOPK = 16
PEER_BLOCK = 128
NORM_EPS = 1e-6
NEG_BIG = -1e30
FORCE_SCORE = 1e9

VMEM_LIMIT_BYTES = 56 * 1024 * 1024
MXU_DTYPE = jnp.bfloat16


def _mm_kernel(a_ref, b_ref, o_ref):
    a = a_ref[...].astype(MXU_DTYPE)
    b = b_ref[...].astype(MXU_DTYPE)
    o_ref[...] = jnp.dot(a, b, preferred_element_type=jnp.float32)


def _pick_tm(m):
    for tm in (512, 256, 128):
        if m % tm == 0:
            return tm
    return m


def mm(a, b, tn=512):
    m, k = a.shape
    _, n = b.shape
    tm = _pick_tm(m)
    tn = min(tn, n)
    return pl.pallas_call(
        _mm_kernel,
        out_shape=jax.ShapeDtypeStruct((m, n), jnp.float32),
        grid=(m // tm, pl.cdiv(n, tn)),
        in_specs=[pl.BlockSpec((tm, k), lambda i, j: (i, 0)),
                  pl.BlockSpec((k, tn), lambda i, j: (0, j))],
        out_specs=pl.BlockSpec((tm, tn), lambda i, j: (i, j)),
        compiler_params=pltpu.CompilerParams(
            dimension_semantics=("parallel", "arbitrary"),
            vmem_limit_bytes=VMEM_LIMIT_BYTES),
        name="mm",
    )(a, b)


def mm3(x, w):
    b, t, k = x.shape
    return mm(x.reshape(b * t, k), w).reshape(b, t, w.shape[1])


def rms_norm(x, g):
    xf = x.astype(jnp.float32)
    y = xf * lax.rsqrt(jnp.mean(xf * xf, axis=-1, keepdims=True) + NORM_EPS)
    return (y * g.astype(jnp.float32)).astype(x.dtype)


def masked_softmax(s, mask):
    s = s.astype(jnp.float32)
    m = jnp.max(jnp.where(mask, s, NEG_BIG), axis=-1, keepdims=True)
    e = jnp.where(mask, jnp.exp(s - m), 0.0)
    return e / jnp.maximum(jnp.sum(e, axis=-1, keepdims=True), 1e-30)


def rope_partial(x, pos):
    half = ROT_DIM // 2
    freqs = ROPE_THETA ** (-jnp.arange(half, dtype=jnp.float32) / half)
    ang = pos.astype(jnp.float32)[:, None] * freqs[None, :]
    cos = jnp.cos(ang)[None, :, None, :]
    sin = jnp.sin(ang)[None, :, None, :]
    xr = x[..., :ROT_DIM].astype(jnp.float32)
    x1, x2 = xr[..., :half], xr[..., half:]
    rot = jnp.concatenate([x1 * cos - x2 * sin, x1 * sin + x2 * cos], axis=-1)
    return jnp.concatenate([rot.astype(x.dtype), x[..., ROT_DIM:]], axis=-1)


def adaln(c_prompt, c_sample, w_ada, b_ada):
    bp, bs = c_prompt.shape[0], c_sample.shape[0]
    rows = bp + bs
    pad = (-rows) % 8
    c_all = jnp.concatenate([c_prompt, c_sample, jnp.zeros((pad, c_prompt.shape[1]), c_prompt.dtype)], axis=0)
    mod = mm(jax.nn.silu(c_all), w_ada) + b_ada
    return jnp.split(mod[:bp], 6, axis=-1), jnp.split(mod[bp:rows], 6, axis=-1)


def modulate(x, g, shift, scale):
    return rms_norm(x, g) * (1.0 + scale[:, None, :]) + shift[:, None, :]


def mixer_front(x, shift, scale, norm_g, w_in, q_norm, k_norm, gmlp_norm, pos):
    b, t, _ = x.shape
    h = modulate(x, norm_g, shift, scale)
    z = mm3(h, w_in)
    sizes = [BRANCH_WIDTH, 6 * KV_WIDTH, 3 * N_HEADS, GMLP_WIDTH, GMLP_WIDTH, 2 * CONV_WIDTH]
    zq, zkv, zg, zu, zv, zglu = jnp.split(z, np.cumsum(sizes)[:-1].tolist(), axis=-1)
    q = rms_norm(zq.reshape(b, t, N_HEADS, HEAD_DIM), q_norm)
    q_n = q.reshape(b, t, KV_HEADS, Q_PER_KV, HEAD_DIM)
    q_r = rope_partial(q, pos).reshape(b, t, KV_HEADS, Q_PER_KV, HEAD_DIM)
    kv = zkv.reshape(b, t, 3, 2, KV_HEADS, HEAD_DIM)
    cmp_kv = kv[:, :, 0]
    ks = rope_partial(rms_norm(kv[:, :, 1, 0], k_norm[1]), pos)
    slc_kv = jnp.stack([ks, kv[:, :, 1, 1]], axis=2)
    kw = rope_partial(rms_norm(kv[:, :, 2, 0], k_norm[2]), pos)
    win_kv = jnp.stack([kw, kv[:, :, 2, 1]], axis=2)
    gates = jax.nn.sigmoid(zg.reshape(b, t, KV_HEADS, Q_PER_KV, 3))
    u = jax.nn.gelu(zu)
    v = rms_norm(jax.nn.gelu(zv), gmlp_norm)
    return h, q_r, q_n, gates, zg, cmp_kv, slc_kv, win_kv, u, v, zglu


def compress(cmp_kv, pe, w1, b1, w2, kn):
    b, L = cmp_kv.shape[:2]
    n_seg = L // CMP_STRIDE
    nb = n_seg - CMP_RATIO + 1
    seg = cmp_kv[:, :n_seg * CMP_STRIDE].reshape(b, n_seg, CMP_STRIDE, 2, KV_HEADS, HEAD_DIM)
    w1r = w1.reshape(2, CMP_RATIO, CMP_STRIDE, HEAD_DIM, HEAD_DIM)
    acc = (jnp.einsum('cld,cldh->ch', pe, w1) + b1)[:, None, :]
    for r in range(CMP_RATIO):
        acc = acc + jnp.einsum('bnlcgd,cldh->bncgh', seg[:, r:r + nb], w1r[:, r])
    out = jnp.einsum('bncgh,chd->bncgd', jax.nn.gelu(acc), w2)
    kc = rms_norm(out[:, :, 0], kn)
    vc = out[:, :, 1]
    c_end = jnp.asarray(np.arange(nb) * CMP_STRIDE + CMP_BLOCK - 1, dtype=jnp.int32)
    return kc, vc, c_end


def slc_blocks(slc_kv):
    b, L = slc_kv.shape[:2]
    ns = -(-L // SLC_BLOCK)
    kv = jnp.pad(slc_kv, ((0, 0), (0, ns * SLC_BLOCK - L), (0, 0), (0, 0), (0, 0)))
    kv = kv.reshape(b, ns, SLC_BLOCK, 2, KV_HEADS, HEAD_DIM).transpose(3, 0, 4, 1, 2, 5)
    return kv[0], kv[1]


def overlap_matrix_np(nb, ns):
    cs = np.arange(nb) * CMP_STRIDE
    ce = cs + CMP_BLOCK - 1
    ss = np.arange(ns) * SLC_BLOCK
    se = ss + SLC_BLOCK - 1
    return ((cs[:, None] <= se[None, :]) & (ce[:, None] >= ss[None, :])).astype(np.float32)


def overlap_matrix(nb, ns):
    return jnp.asarray(overlap_matrix_np(nb, ns))


def nsa_attend(q_r, q_n, gates, pos_q, kc, vc, c_end, ovl, sk, sv, wk, wv, w_pos):
    b, tq = q_r.shape[:2]
    scale = HEAD_DIM ** -0.5
    s = jnp.einsum('bqgrd,bngd->bqgrn', q_n, kc) * scale
    p = masked_softmax(s, (c_end[None, :] <= pos_q[:, None])[None, :, None, None, :])
    o_c = jnp.einsum('bqgrn,bngd->bqgrd', p.astype(vc.dtype), vc)
    imp = jnp.einsum('bqgrn,ns->bgqs', p, ovl)
    ns = ovl.shape[1]
    j = jnp.arange(ns, dtype=jnp.int32)[None, :]
    cur = (pos_q // SLC_BLOCK)[:, None]
    valid = j <= cur
    forced = (j == 0) | (j == cur) | (j == cur - 1)
    imp = jnp.where(valid[None, None], jnp.where(forced[None, None], FORCE_SCORE, imp), -FORCE_SCORE)
    n_sel = min(N_SELECT, ns)
    sel = lax.top_k(imp, n_sel)[1]
    gather = jax.vmap(jax.vmap(lambda kb, si: kb[si]))
    ksel = gather(sk, sel)
    vsel = gather(sv, sel)
    tok = sel[..., None] * SLC_BLOCK + jnp.arange(SLC_BLOCK, dtype=jnp.int32)
    ms = (tok <= pos_q[None, None, :, None, None]).transpose(0, 2, 1, 3, 4).reshape(b, tq, KV_HEADS, 1, n_sel * SLC_BLOCK)
    s2 = jnp.einsum('bqgrd,bgqnld->bqgrnl', q_r, ksel).reshape(b, tq, KV_HEADS, Q_PER_KV, n_sel * SLC_BLOCK) * scale
    p2 = masked_softmax(s2, ms)
    o_s = jnp.einsum('bqgrk,bgqkd->bqgrd', p2.astype(vsel.dtype), vsel.reshape(b, KV_HEADS, tq, n_sel * SLC_BLOCK, HEAD_DIM))
    s3 = jnp.einsum('bqgrd,bkgd->bqgrk', q_r, wk) * scale
    dlt = pos_q[:, None] - w_pos[None, :]
    mw = (dlt >= 0) & (dlt < WINDOW) & (w_pos[None, :] >= 0)
    p3 = masked_softmax(s3, mw[None, :, None, None, :])
    o_w = jnp.einsum('bqgrk,bkgd->bqgrd', p3.astype(wv.dtype), wv)
    return gates[..., 0:1] * o_c + gates[..., 1:2] * o_s + gates[..., 2:3] * o_w


NSA_KT = 1024
_NT_DIMS = (((1,), (1,)), ((), ()))


def _softmax_tile_update(s, mask, v, sl, m_scr, l_scr, acc_scr):
    m_old = m_scr[sl]
    m_new = jnp.maximum(m_old, jnp.max(jnp.where(mask, s, NEG_BIG), axis=-1, keepdims=True))
    alpha = jnp.exp(m_old - m_new)
    e = jnp.where(mask, jnp.exp(s - m_new), 0.0)
    l_scr[sl] = alpha * l_scr[sl] + jnp.sum(e, axis=-1, keepdims=True)
    acc_scr[sl] = alpha * acc_scr[sl] + jnp.dot(e.astype(MXU_DTYPE), v, preferred_element_type=jnp.float32)
    m_scr[sl] = m_new


def _nsa_prompt_kernel(qn_ref, qr_ref, gate_ref, kc_ref, vc_ref, ovl_ref, eexp_ref,
                       ks_ref, vs_ref, kw_ref, vw_ref, o_ref, m_scr, l_scr, acc_scr):
    qi = pl.program_id(2)
    q0 = qi * Q_BLOCK
    rows = Q_PER_KV * Q_BLOCK
    scale = HEAD_DIM ** -0.5
    qn = (qn_ref[0, 0].reshape(rows, HEAD_DIM) * scale).astype(MXU_DTYPE)
    qr = (qr_ref[0, 0].reshape(rows, HEAD_DIM) * scale).astype(MXU_DTYPE)
    pos_row = q0 + (lax.broadcasted_iota(jnp.int32, (rows, 1), 0) & (Q_BLOCK - 1))

    kc = kc_ref[0, 0]
    ncp = kc.shape[0]
    s = lax.dot_general(qn, kc, _NT_DIMS, preferred_element_type=jnp.float32)
    c_end = lax.broadcasted_iota(jnp.int32, (rows, ncp), 1) * CMP_STRIDE + (CMP_BLOCK - 1)
    cmask = c_end <= pos_row
    m = jnp.max(jnp.where(cmask, s, NEG_BIG), axis=-1, keepdims=True)
    e = jnp.where(cmask, jnp.exp(s - m), 0.0)
    p = e / jnp.maximum(jnp.sum(e, axis=-1, keepdims=True), 1e-30)
    o_c = jnp.dot(p.astype(MXU_DTYPE), vc_ref[0, 0], preferred_element_type=jnp.float32)

    psum = p[0:Q_BLOCK]
    for r in range(1, Q_PER_KV):
        psum = psum + p[r * Q_BLOCK:(r + 1) * Q_BLOCK]
    p_hi = psum.astype(MXU_DTYPE)
    p_lo = (psum - p_hi.astype(jnp.float32)).astype(MXU_DTYPE)
    ovl = ovl_ref[...]
    imp = (jnp.dot(p_hi, ovl, preferred_element_type=jnp.float32)
           + jnp.dot(p_lo, ovl, preferred_element_type=jnp.float32))
    ns = imp.shape[1]
    j = lax.broadcasted_iota(jnp.int32, (Q_BLOCK, ns), 1)
    cur = (q0 + lax.broadcasted_iota(jnp.int32, (Q_BLOCK, 1), 0)) // SLC_BLOCK
    forced = (j == 0) | (j == cur) | (j == cur - 1)
    adj = jnp.where(j <= cur, jnp.where(forced, FORCE_SCORE, imp), -FORCE_SCORE)
    rank = jnp.zeros((Q_BLOCK, ns), jnp.int32)
    for k in range(ns):
        col = adj[:, k:k + 1]
        beats = (col > adj) | ((col == adj) & (j > k))
        rank = rank + beats.astype(jnp.int32)
    sel = (rank < N_SELECT).astype(MXU_DTYPE)

    def init():
        m_scr[...] = jnp.full(m_scr.shape, NEG_BIG, jnp.float32)
        l_scr[...] = jnp.zeros(l_scr.shape, jnp.float32)
        acc_scr[...] = jnp.zeros(acc_scr.shape, jnp.float32)

    def finish():
        return acc_scr[...] / jnp.maximum(l_scr[...], 1e-30)

    init()
    pos_q = pos_row[0:Q_BLOCK]
    all_rows = slice(0, rows)

    def slc_body(kt, carry):
        k0 = pl.multiple_of(kt * NSA_KT, NSA_KT)
        s2 = lax.dot_general(qr, ks_ref[0, 0, pl.ds(k0, NSA_KT), :], _NT_DIMS,
                             preferred_element_type=jnp.float32)
        bm = jnp.dot(sel, eexp_ref[kt], preferred_element_type=jnp.float32)
        kpos = k0 + lax.broadcasted_iota(jnp.int32, (Q_BLOCK, NSA_KT), 1)
        mask = (bm > 0.5) & (kpos <= pos_q)
        mask = jnp.concatenate([mask] * Q_PER_KV, axis=0)
        _softmax_tile_update(s2, mask, vs_ref[0, 0, pl.ds(k0, NSA_KT), :], all_rows, m_scr, l_scr, acc_scr)
        return carry

    lax.fori_loop(0, (q0 + Q_BLOCK + NSA_KT - 1) // NSA_KT, slc_body, 0)
    o_s = finish()

    init()
    w0 = pl.multiple_of(jnp.maximum(q0 - WINDOW, 0), Q_BLOCK)
    span = WINDOW + Q_BLOCK
    s3 = lax.dot_general(qr, kw_ref[0, 0, pl.ds(w0, span), :], _NT_DIMS,
                         preferred_element_type=jnp.float32)
    dlt = pos_q - (w0 + lax.broadcasted_iota(jnp.int32, (Q_BLOCK, span), 1))
    wmask = (dlt >= 0) & (dlt < WINDOW)
    wmask = jnp.concatenate([wmask] * Q_PER_KV, axis=0)
    _softmax_tile_update(s3, wmask, vw_ref[0, 0, pl.ds(w0, span), :], all_rows, m_scr, l_scr, acc_scr)
    o_w = finish()

    gate = jax.nn.sigmoid(gate_ref[0, 0])
    outs = []
    for r in range(Q_PER_KV):
        sl = slice(r * Q_BLOCK, (r + 1) * Q_BLOCK)
        outs.append(gate[:, 3 * r:3 * r + 1] * o_c[sl] + gate[:, 3 * r + 1:3 * r + 2] * o_s[sl]
                    + gate[:, 3 * r + 2:3 * r + 3] * o_w[sl])
    o_ref[0] = jnp.concatenate(outs, axis=1)


def nsa_prompt(q_r, q_n, zg, kc, vc, slc_kv, win_kv):
    b, t = q_r.shape[:2]
    assert t % NSA_KT == 0 and t >= WINDOW + Q_BLOCK and WINDOW % Q_BLOCK == 0
    nb = kc.shape[1]
    ncp = -(-nb // 128) * 128
    ns = t // SLC_BLOCK
    heads_first = lambda a: a.transpose(0, 2, 3, 1, 4)
    kv_first = lambda a: a.transpose(0, 2, 1, 3).astype(MXU_DTYPE)
    pad_c = lambda a: jnp.pad(a, ((0, 0), (0, ncp - nb), (0, 0), (0, 0)))
    gate_in = zg.reshape(b, t, KV_HEADS, 3 * Q_PER_KV).transpose(0, 2, 1, 3)
    ovl = np.zeros((ncp, ns), np.float32)
    ovl[:nb] = np.asarray(overlap_matrix_np(nb, ns))
    key_blk = (np.arange(t) // SLC_BLOCK).reshape(t // NSA_KT, 1, NSA_KT)
    eexp = (key_blk == np.arange(ns).reshape(1, ns, 1)).astype(np.float32)
    rows = Q_PER_KV * Q_BLOCK
    q_spec = pl.BlockSpec((1, 1, Q_PER_KV, Q_BLOCK, HEAD_DIM), lambda bi, g, qi: (bi, g, 0, qi, 0))
    c_spec = pl.BlockSpec((1, 1, ncp, HEAD_DIM), lambda bi, g, qi: (bi, g, 0, 0))
    kv_spec = pl.BlockSpec((1, 1, t, HEAD_DIM), lambda bi, g, qi: (bi, g, 0, 0))
    return pl.pallas_call(
        _nsa_prompt_kernel,
        out_shape=jax.ShapeDtypeStruct((b, t, BRANCH_WIDTH), jnp.float32),
        grid=(b, KV_HEADS, t // Q_BLOCK),
        in_specs=[q_spec, q_spec,
                  pl.BlockSpec((1, 1, Q_BLOCK, 3 * Q_PER_KV), lambda bi, g, qi: (bi, g, qi, 0)),
                  c_spec, c_spec,
                  pl.BlockSpec((ncp, ns), lambda bi, g, qi: (0, 0)),
                  pl.BlockSpec((t // NSA_KT, ns, NSA_KT), lambda bi, g, qi: (0, 0, 0)),
                  kv_spec, kv_spec, kv_spec, kv_spec],
        out_specs=pl.BlockSpec((1, Q_BLOCK, Q_PER_KV * HEAD_DIM), lambda bi, g, qi: (bi, qi, g)),
        scratch_shapes=[pltpu.VMEM((rows, 1), jnp.float32),
                        pltpu.VMEM((rows, 1), jnp.float32),
                        pltpu.VMEM((rows, HEAD_DIM), jnp.float32)],
        compiler_params=pltpu.CompilerParams(
            dimension_semantics=("parallel", "parallel", "arbitrary"),
            vmem_limit_bytes=VMEM_LIMIT_BYTES),
        name="nsa_prompt",
    )(heads_first(q_n), heads_first(q_r), gate_in,
      kv_first(pad_c(kc)), kv_first(pad_c(vc)),
      jnp.asarray(ovl, MXU_DTYPE), jnp.asarray(eexp, MXU_DTYPE),
      kv_first(slc_kv[:, :, 0]), kv_first(slc_kv[:, :, 1]),
      kv_first(win_kv[:, :, 0]), kv_first(win_kv[:, :, 1]))


def gmlp_mix(u, v, w_s, b_s):
    b, t, _ = v.shape
    nc = -(-t // CHUNK)
    vp = jnp.pad(v, ((0, 0), (0, nc * CHUNK - t), (0, 0))).reshape(b, nc, CHUNK, GMLP_GROUPS, GMLP_WIDTH // GMLP_GROUPS)
    w = w_s * jnp.tril(jnp.ones((CHUNK, CHUNK), w_s.dtype))
    mixed = jnp.einsum('gpq,bcqgd->bcpgd', w, vp) + b_s.T[None, None, :, :, None]
    return u * mixed.reshape(b, nc * CHUNK, GMLP_WIDTH)[:, :t]


def conformer_conv(zglu, buf, w_dw, b_dw, norm_g):
    a, gt = jnp.split(zglu, 2, axis=-1)
    glu = a * jax.nn.sigmoid(gt)
    xin = jnp.concatenate([buf.astype(glu.dtype), glu], axis=1)
    y = lax.conv_general_dilated(xin, w_dw[:, None, :].astype(glu.dtype), (1,), 'VALID',
                                 dimension_numbers=('NWC', 'WIO', 'NWC'), feature_group_count=CONV_WIDTH) + b_dw
    y = jax.nn.silu(rms_norm(y, norm_g))
    return y, xin[:, xin.shape[1] - (CONV_K - 1):]


def mixer_merge(x, h, o_nsa, o_gmlp, o_conv, gate, w_branch, w_merge, b_merge, w_out):
    b, t, d = x.shape
    mg = jax.nn.sigmoid(mm3(h, w_merge) + b_merge).reshape(b, t, 3, d)
    y = (mg[:, :, 0] * mm3(o_nsa, w_branch[0]) + mg[:, :, 1] * mm3(o_gmlp, w_branch[1])
         + mg[:, :, 2] * mm3(o_conv, w_branch[2]))
    return x + gate[:, None, :] * mm3(y, w_out)


PEER_TB = 512
PEER_I1 = 8
PEER_NEG = -3.0e38
PEER_NRANK = PEER_TOPK + 1
PEER_VROWS = 24
PEER_NCAND = PEER_VROWS + 7 * 8 + 16


def _extract_top(cur_ref, out_ref, n_out):
    n = cur_ref.shape[0]
    iota = lax.broadcasted_iota(jnp.int32, cur_ref.shape, 0)
    for k in range(n_out):
        cur = cur_ref[...]
        m = jnp.max(cur, axis=0, keepdims=True)
        first = jnp.min(jnp.where(cur == m, iota, n), axis=0, keepdims=True)
        cur_ref[...] = jnp.where(iota == first, PEER_NEG, cur)
        out_ref[k:k + 1, :] = m


def _peer_route_kernel(x_ref, w_ref, k_ref, thr1_ref, e1_ref, s2_ref, e2_ref,
                       work_ref, v1_ref, v2_ref, cand_ref, ctop_ref):
    x = x_ref[...].astype(MXU_DTYPE)
    w = w_ref[...].astype(MXU_DTYPE)
    q = jnp.dot(x, w, preferred_element_type=jnp.float32)
    half = PEER_QDIM // 2
    nt = (((1,), (1,)), ((), ()))
    s = []
    for c in range(2):
        qc = q[:, c * half:(c + 1) * half].astype(MXU_DTYPE)
        kc = k_ref[0, c].astype(MXU_DTYPE)
        s.append(lax.dot_general(kc, qc, nt, preferred_element_type=jnp.float32))
    s1, s2 = s
    v1_ref[...] = jnp.full(v1_ref.shape, PEER_NEG, jnp.float32)
    v2_ref[...] = jnp.full(v2_ref.shape, PEER_NEG, jnp.float32)
    work_ref[...] = s1
    _extract_top(work_ref, v1_ref, PEER_NRANK)
    work_ref[...] = s2
    _extract_top(work_ref, v2_ref, PEER_NRANK)
    v1 = v1_ref[...]
    v2 = v2_ref[...]
    nv = PEER_VROWS
    cand_ref[0:nv, :] = v1[0:1] + v2
    for a in range(1, 8):
        cand_ref[nv + 8 * (a - 1):nv + 8 * a, :] = v1[a:a + 1] + v2[0:8]
    cand_ref[nv + 56:nv + 72, :] = v1[8:nv] + v2[0:1]
    _extract_top(cand_ref, ctop_ref, PEER_NRANK)
    tau = 0.5 * (ctop_ref[PEER_TOPK - 1:PEER_TOPK, :] + ctop_ref[PEER_TOPK:PEER_TOPK + 1, :])
    m1 = v1[0:1]
    m2 = v2[0:1]
    e1top = jnp.exp(v1 - m1)
    e2top = jnp.exp(v2 - m2)
    z = jnp.zeros_like(tau)
    for a in range(PEER_NRANK):
        sel = v2 >= (tau - v1[a:a + 1])
        z = z + jnp.sum(jnp.where(sel, e2top, 0.0), axis=0, keepdims=True) * e1top[a:a + 1]
    inv_z = 1.0 / z
    thr1_ref[0] = tau - s1
    e1_ref[0] = jnp.exp(s1 - m1) * inv_z
    s2_ref[0] = s2
    e2_ref[0] = jnp.exp(s2 - m2)


def peer_route(xm, w_pq, sub_keys):
    t, d = xm.shape
    tb = PEER_TB
    out = jax.ShapeDtypeStruct((PEER_HEADS, PEER_KEYS, t), jnp.float32)
    ospec = pl.BlockSpec((1, PEER_KEYS, tb), lambda i, h: (h, 0, i))
    return pl.pallas_call(
        _peer_route_kernel,
        out_shape=(out, out, out, out),
        grid=(t // tb, PEER_HEADS),
        in_specs=[pl.BlockSpec((tb, d), lambda i, h: (i, 0)),
                  pl.BlockSpec((d, PEER_QDIM), lambda i, h: (0, h)),
                  pl.BlockSpec((1, 2, PEER_KEYS, PEER_QDIM // 2), lambda i, h: (h, 0, 0, 0))],
        out_specs=(ospec, ospec, ospec, ospec),
        scratch_shapes=[pltpu.VMEM((PEER_KEYS, tb), jnp.float32),
                        pltpu.VMEM((PEER_VROWS, tb), jnp.float32),
                        pltpu.VMEM((PEER_VROWS, tb), jnp.float32),
                        pltpu.VMEM((PEER_NCAND, tb), jnp.float32),
                        pltpu.VMEM((PEER_VROWS, tb), jnp.float32)],
        compiler_params=pltpu.CompilerParams(
            dimension_semantics=("parallel", "arbitrary"),
            vmem_limit_bytes=VMEM_LIMIT_BYTES),
        name="peer_route",
    )(xm, w_pq, sub_keys)


def _peer_dense_kernel(xt_ref, u_ref, vt_ref, thr1_ref, e1_ref, s2_ref, e2_ref, o_ref, s_scr, w_scr):
    @pl.when(pl.program_id(1) == 0)
    def _():
        o_ref[...] = jnp.zeros_like(o_ref)

    s_scr[...] = jnp.dot(u_ref[...], xt_ref[...], preferred_element_type=jnp.float32)
    tb = o_ref.shape[1]
    nk = PEER_KEYS
    for j in range(PEER_I1):
        for t0 in range(0, tb, 128):
            g = jnp.zeros((nk, 128), jnp.float32)
            for h in range(PEER_HEADS):
                thr = thr1_ref[h, j:j + 1, t0:t0 + 128]
                e1 = e1_ref[h, j:j + 1, t0:t0 + 128]
                sel = s2_ref[h, :, t0:t0 + 128] >= thr
                g = g + jnp.where(sel, e2_ref[h, :, t0:t0 + 128], 0.0) * e1
            act = jax.nn.gelu(s_scr[j * nk:(j + 1) * nk, t0:t0 + 128])
            w_scr[j * nk:(j + 1) * nk, t0:t0 + 128] = (act * g).astype(MXU_DTYPE)
    o_ref[...] += jnp.dot(vt_ref[...], w_scr[...], preferred_element_type=jnp.float32)


def peer_dense(xt, u_bf, vt_bf, thr1, e1, s2, e2):
    d, t = xt.shape
    n_exp = u_bf.shape[0]
    tb = PEER_TB
    ec = PEER_I1 * PEER_KEYS
    row_spec = pl.BlockSpec((PEER_HEADS, PEER_I1, tb), lambda i, c: (0, c, i))
    full_spec = pl.BlockSpec((PEER_HEADS, PEER_KEYS, tb), lambda i, c: (0, 0, i))
    return pl.pallas_call(
        _peer_dense_kernel,
        out_shape=jax.ShapeDtypeStruct((d, t), jnp.float32),
        grid=(t // tb, n_exp // ec),
        in_specs=[pl.BlockSpec((d, tb), lambda i, c: (0, i)),
                  pl.BlockSpec((ec, d), lambda i, c: (c, 0)),
                  pl.BlockSpec((d, ec), lambda i, c: (0, c)),
                  row_spec, row_spec, full_spec, full_spec],
        out_specs=pl.BlockSpec((d, tb), lambda i, c: (0, i)),
        scratch_shapes=[pltpu.VMEM((ec, tb), jnp.float32),
                        pltpu.VMEM((ec, tb), MXU_DTYPE)],
        compiler_params=pltpu.CompilerParams(
            dimension_semantics=("parallel", "arbitrary"),
            vmem_limit_bytes=VMEM_LIMIT_BYTES),
        name="peer_dense",
    )(xt, u_bf, vt_bf, thr1, e1, s2, e2)


def peer_ffn(xm, w_pq, sub_keys, u_bf, vt_bf):
    t, d = xm.shape
    tp = -(-t // PEER_TB) * PEER_TB
    xm_p = jnp.pad(xm, ((0, tp - t), (0, 0)))
    thr1, e1, s2, e2 = peer_route(xm_p, w_pq, sub_keys)
    out_t = peer_dense(xm_p.T.astype(MXU_DTYPE), u_bf, vt_bf, thr1, e1, s2, e2)
    return out_t.T[:t]


def kernel(x_prompt, x_sample, cache_cmp_kv, cache_slc_kv, state_win_kv, state_conv, page_table,
           c_prompt, c_sample, w_ada, b_ada, norm_mix, norm_ffn, w_in, q_norm, k_norm,
           cmp_pe, cmp_w1, cmp_b1, cmp_w2, gmlp_norm, gmlp_ws, gmlp_bs, conv_w, conv_b, conv_norm,
           w_branch, w_merge, b_merge, w_out, peer_wq, peer_keys, peer_u, peer_v):
    bp, t = x_prompt.shape[:2]
    bs, ds = x_sample.shape[:2]
    depth = w_in.shape[0]
    past_len = page_table.shape[1] * PAGE_SIZE
    pos_p = jnp.arange(t, dtype=jnp.int32)
    pos_s = past_len + jnp.arange(ds, dtype=jnp.int32)
    xp, xs = x_prompt, x_sample
    cmp_p, cmp_s, slc_p, slc_s, win_p, win_s, conv_p, conv_s, gv_s = [], [], [], [], [], [], [], [], []
    for l in range(depth):
        mp, msm = adaln(c_prompt, c_sample, w_ada[l], b_ada[l])
        hp, q_r, q_n, gates, zg, ckv, skv, wkv, u, v, zglu = mixer_front(
            xp, mp[0], mp[1], norm_mix[l], w_in[l], q_norm[l], k_norm[l], gmlp_norm[l], pos_p)
        kc, vc, c_end = compress(ckv, cmp_pe[l], cmp_w1[l], cmp_b1[l], cmp_w2[l], k_norm[l, 0])
        o_nsa = nsa_prompt(q_r, q_n, zg, kc, vc, skv, wkv)
        o_g = gmlp_mix(u, v, gmlp_ws[l], gmlp_bs[l])
        o_c, buf = conformer_conv(zglu, jnp.zeros((bp, CONV_K - 1, CONV_WIDTH), zglu.dtype),
                                  conv_w[l], conv_b[l], conv_norm[l])
        xp = mixer_merge(xp, hp, o_nsa, o_g, o_c, mp[2], w_branch[l], w_merge[l], b_merge[l], w_out[l])
        cmp_p.append(ckv)
        slc_p.append(skv)
        win_p.append(wkv[:, t - min(WINDOW, t):])
        conv_p.append(buf)
        hs, q_r, q_n, gates, zg, ckv, skv, wkv, u, v, zglu = mixer_front(
            xs, msm[0], msm[1], norm_mix[l], w_in[l], q_norm[l], k_norm[l], gmlp_norm[l], pos_s)
        past_c = cache_cmp_kv[l][page_table].reshape(bs, past_len, 2, KV_HEADS, HEAD_DIM)
        past_s = cache_slc_kv[l][page_table].reshape(bs, past_len, 2, KV_HEADS, HEAD_DIM)
        kc, vc, c_end = compress(jnp.concatenate([past_c, ckv], axis=1),
                                 cmp_pe[l], cmp_w1[l], cmp_b1[l], cmp_w2[l], k_norm[l, 0])
        sk, sv = slc_blocks(jnp.concatenate([past_s, skv], axis=1))
        ovl = overlap_matrix(kc.shape[1], sk.shape[2])
        wb = state_win_kv.shape[2]
        win_full = jnp.concatenate([state_win_kv[l], wkv], axis=1)
        w_pos = past_len - wb + jnp.arange(wb + ds, dtype=jnp.int32)
        o_nsa = nsa_attend(q_r, q_n, gates, pos_s, kc, vc, c_end, ovl, sk, sv,
                           win_full[:, :, 0], win_full[:, :, 1], w_pos).reshape(bs, ds, BRANCH_WIDTH)
        o_g = gmlp_mix(u, v, gmlp_ws[l], gmlp_bs[l])
        o_c, buf = conformer_conv(zglu, state_conv[l], conv_w[l], conv_b[l], conv_norm[l])
        xs = mixer_merge(xs, hs, o_nsa, o_g, o_c, msm[2], w_branch[l], w_merge[l], b_merge[l], w_out[l])
        d = xp.shape[-1]
        xm_all = jnp.concatenate([modulate(xp, norm_ffn[l], mp[3], mp[4]).reshape(bp * t, d),
                                  modulate(xs, norm_ffn[l], msm[3], msm[4]).reshape(bs * ds, d)], axis=0)
        ffn = peer_ffn(xm_all, peer_wq[l], peer_keys[l],
                       peer_u[l].astype(MXU_DTYPE), peer_v[l].T.astype(MXU_DTYPE))
        xp = xp + mp[5][:, None, :] * ffn[:bp * t].reshape(bp, t, d)
        xs = xs + msm[5][:, None, :] * ffn[bp * t:].reshape(bs, ds, d)
        cmp_s.append(ckv)
        slc_s.append(skv)
        lw = win_full.shape[1]
        win_s.append(win_full[:, lw - min(WINDOW, lw):])
        conv_s.append(buf)
        gv_s.append(v)
    return (xp, xs, jnp.stack(cmp_p), jnp.stack(cmp_s), jnp.stack(slc_p), jnp.stack(slc_s),
            jnp.stack(win_p), jnp.stack(win_s), jnp.stack(conv_p), jnp.stack(conv_s), jnp.stack(gv_s))
```

```python
import functools

import jax
import jax.numpy as jnp
import numpy as np
from jax import lax
from jax.experimental import pallas as pl
from jax.experimental.pallas import tpu as pltpu

D_MODEL = 2048
PAGE_SIZE = 128
BRANCH_WIDTH = D_MODEL // 2
HEAD_DIM = 64
N_HEADS = BRANCH_WIDTH // HEAD_DIM
KV_HEADS = 4
Q_PER_KV = N_HEADS // KV_HEADS
KV_WIDTH = KV_HEADS * HEAD_DIM
ROT_DIM = HEAD_DIM // 4
ROPE_THETA = 500000.0
CMP_BLOCK = 32
CMP_STRIDE = 16
CMP_RATIO = CMP_BLOCK // CMP_STRIDE
SLC_BLOCK = 64
N_SELECT = 16
WINDOW = 512
Q_BLOCK = 128
GMLP_WIDTH = BRANCH_WIDTH
GMLP_GROUPS = 8
CHUNK = 128
CONV_WIDTH = BRANCH_WIDTH
CONV_K = 31
PEER_HEADS = 8
PEER_KEYS = 128
PEER_QDIM = 256
PEER_TOPK = 16
PEER_BLOCK = 128
NORM_EPS = 1e-6
NEG_BIG = -1e30
FORCE_SCORE = 1e9

VMEM_LIMIT_BYTES = 56 * 1024 * 1024
MXU_DTYPE = jnp.bfloat16


def _mm_kernel(a_ref, b_ref, o_ref):
    o_ref[...] = jnp.dot(a_ref[...].astype(MXU_DTYPE), b_ref[...].astype(MXU_DTYPE),
                         preferred_element_type=jnp.float32)


def _pick_tm(m):
    for tm in (1024, 512, 256, 128):
        if m % tm == 0:
            return tm
    return m


def mm(a, b, tn=1024):
    m, k = a.shape
    _, n = b.shape
    tm = _pick_tm(m)
    tn = min(tn, n)
    if jnp.dtype(b.dtype).itemsize > 2:
        tn = min(tn, 512)
    return pl.pallas_call(
        _mm_kernel,
        out_shape=jax.ShapeDtypeStruct((m, n), jnp.float32),
        grid=(m // tm, pl.cdiv(n, tn)),
        in_specs=[pl.BlockSpec((tm, k), lambda i, j: (i, 0)),
                  pl.BlockSpec((k, tn), lambda i, j: (0, j))],
        out_specs=pl.BlockSpec((tm, tn), lambda i, j: (i, j)),
        compiler_params=pltpu.CompilerParams(
            dimension_semantics=("parallel", "arbitrary"),
            vmem_limit_bytes=VMEM_LIMIT_BYTES),
        name="mm",
    )(a, b)


def mm3(x, w):
    b, t, k = x.shape
    return mm(x.reshape(b * t, k), w).reshape(b, t, w.shape[1])


def rms_norm(x, g):
    xf = x.astype(jnp.float32)
    y = xf * lax.rsqrt(jnp.mean(xf * xf, axis=-1, keepdims=True) + NORM_EPS)
    return (y * g.astype(jnp.float32)).astype(x.dtype)


def masked_softmax(s, mask):
    s = s.astype(jnp.float32)
    m = jnp.max(jnp.where(mask, s, NEG_BIG), axis=-1, keepdims=True)
    e = jnp.where(mask, jnp.exp(s - m), 0.0)
    return e / jnp.maximum(jnp.sum(e, axis=-1, keepdims=True), 1e-30)


def rope_partial(x, pos):
    half = ROT_DIM // 2
    freqs = ROPE_THETA ** (-jnp.arange(half, dtype=jnp.float32) / half)
    ang = pos.astype(jnp.float32)[:, None] * freqs[None, :]
    cos = jnp.cos(ang)[None, :, None, :]
    sin = jnp.sin(ang)[None, :, None, :]
    xr = x[..., :ROT_DIM].astype(jnp.float32)
    x1, x2 = xr[..., :half], xr[..., half:]
    rot = jnp.concatenate([x1 * cos - x2 * sin, x1 * sin + x2 * cos], axis=-1)
    return jnp.concatenate([rot.astype(x.dtype), x[..., ROT_DIM:]], axis=-1)


def adaln(c_prompt, c_sample, w_ada, b_ada):
    bp, bs = c_prompt.shape[0], c_sample.shape[0]
    rows = bp + bs
    pad = (-rows) % 8
    c_all = jnp.concatenate([c_prompt, c_sample, jnp.zeros((pad, c_prompt.shape[1]), c_prompt.dtype)], axis=0)
    mod = mm(jax.nn.silu(c_all), w_ada) + b_ada
    return jnp.split(mod[:bp], 6, axis=-1), jnp.split(mod[bp:rows], 6, axis=-1)


def modulate(x, g, shift, scale):
    return rms_norm(x, g) * (1.0 + scale[:, None, :]) + shift[:, None, :]


def mixer_front(x, shift, scale, norm_g, w_in, q_norm, k_norm, gmlp_norm, pos):
    b, t, _ = x.shape
    h = modulate(x, norm_g, shift, scale)
    z = mm3(h, w_in)
    sizes = [BRANCH_WIDTH, 6 * KV_WIDTH, 3 * N_HEADS, GMLP_WIDTH, GMLP_WIDTH, 2 * CONV_WIDTH]
    zq, zkv, zg, zu, zv, zglu = jnp.split(z, np.cumsum(sizes)[:-1].tolist(), axis=-1)
    q = rms_norm(zq.reshape(b, t, N_HEADS, HEAD_DIM), q_norm)
    q_n = q.reshape(b, t, KV_HEADS, Q_PER_KV, HEAD_DIM)
    q_r = rope_partial(q, pos).reshape(b, t, KV_HEADS, Q_PER_KV, HEAD_DIM)
    kv = zkv.reshape(b, t, 3, 2, KV_HEADS, HEAD_DIM)
    cmp_kv = kv[:, :, 0]
    ks = rope_partial(rms_norm(kv[:, :, 1, 0], k_norm[1]), pos)
    slc_kv = jnp.stack([ks, kv[:, :, 1, 1]], axis=2)
    kw = rope_partial(rms_norm(kv[:, :, 2, 0], k_norm[2]), pos)
    win_kv = jnp.stack([kw, kv[:, :, 2, 1]], axis=2)
    gates = jax.nn.sigmoid(zg.reshape(b, t, KV_HEADS, Q_PER_KV, 3))
    u = jax.nn.gelu(zu)
    v = rms_norm(jax.nn.gelu(zv), gmlp_norm)
    return h, q_r, q_n, gates, zg, cmp_kv, slc_kv, win_kv, u, v, zglu


def compress(cmp_kv, pe, w1, b1, w2, kn):
    b, L = cmp_kv.shape[:2]
    n_seg = L // CMP_STRIDE
    nb = n_seg - CMP_RATIO + 1
    seg = cmp_kv[:, :n_seg * CMP_STRIDE].reshape(b, n_seg, CMP_STRIDE, 2, KV_HEADS, HEAD_DIM)
    w1r = w1.reshape(2, CMP_RATIO, CMP_STRIDE, HEAD_DIM, HEAD_DIM)
    acc = (jnp.einsum('cld,cldh->ch', pe, w1) + b1)[:, None, :]
    for r in range(CMP_RATIO):
        acc = acc + jnp.einsum('bnlcgd,cldh->bncgh', seg[:, r:r + nb], w1r[:, r])
    out = jnp.einsum('bncgh,chd->bncgd', jax.nn.gelu(acc), w2)
    kc = rms_norm(out[:, :, 0], kn)
    vc = out[:, :, 1]
    c_end = jnp.asarray(np.arange(nb) * CMP_STRIDE + CMP_BLOCK - 1, dtype=jnp.int32)
    return kc, vc, c_end


def slc_blocks(slc_kv):
    b, L = slc_kv.shape[:2]
    ns = -(-L // SLC_BLOCK)
    kv = jnp.pad(slc_kv, ((0, 0), (0, ns * SLC_BLOCK - L), (0, 0), (0, 0), (0, 0)))
    kv = kv.reshape(b, ns, SLC_BLOCK, 2, KV_HEADS, HEAD_DIM).transpose(3, 0, 4, 1, 2, 5)
    return kv[0], kv[1]


def overlap_matrix_np(nb, ns):
    cs = np.arange(nb) * CMP_STRIDE
    ce = cs + CMP_BLOCK - 1
    ss = np.arange(ns) * SLC_BLOCK
    se = ss + SLC_BLOCK - 1
    return ((cs[:, None] <= se[None, :]) & (ce[:, None] >= ss[None, :])).astype(np.float32)


def overlap_matrix(nb, ns):
    return jnp.asarray(overlap_matrix_np(nb, ns))


def nsa_attend(q_r, q_n, gates, pos_q, kc, vc, c_end, ovl, sk, sv, wk, wv, w_pos):
    b, tq = q_r.shape[:2]
    scale = HEAD_DIM ** -0.5
    s = jnp.einsum('bqgrd,bngd->bqgrn', q_n, kc) * scale
    p = masked_softmax(s, (c_end[None, :] <= pos_q[:, None])[None, :, None, None, :])
    o_c = jnp.einsum('bqgrn,bngd->bqgrd', p.astype(vc.dtype), vc)
    imp = jnp.einsum('bqgrn,ns->bgqs', p, ovl)
    ns = ovl.shape[1]
    j = jnp.arange(ns, dtype=jnp.int32)[None, :]
    cur = (pos_q // SLC_BLOCK)[:, None]
    valid = j <= cur
    forced = (j == 0) | (j == cur) | (j == cur - 1)
    imp = jnp.where(valid[None, None], jnp.where(forced[None, None], FORCE_SCORE, imp), -FORCE_SCORE)
    n_sel = min(N_SELECT, ns)
    sel = lax.top_k(imp, n_sel)[1]
    gather = jax.vmap(jax.vmap(lambda kb, si: kb[si]))
    ksel = gather(sk, sel)
    vsel = gather(sv, sel)
    tok = sel[..., None] * SLC_BLOCK + jnp.arange(SLC_BLOCK, dtype=jnp.int32)
    ms = (tok <= pos_q[None, None, :, None, None]).transpose(0, 2, 1, 3, 4).reshape(b, tq, KV_HEADS, 1, n_sel * SLC_BLOCK)
    s2 = jnp.einsum('bqgrd,bgqnld->bqgrnl', q_r, ksel).reshape(b, tq, KV_HEADS, Q_PER_KV, n_sel * SLC_BLOCK) * scale
    p2 = masked_softmax(s2, ms)
    o_s = jnp.einsum('bqgrk,bgqkd->bqgrd', p2.astype(vsel.dtype), vsel.reshape(b, KV_HEADS, tq, n_sel * SLC_BLOCK, HEAD_DIM))
    s3 = jnp.einsum('bqgrd,bkgd->bqgrk', q_r, wk) * scale
    dlt = pos_q[:, None] - w_pos[None, :]
    mw = (dlt >= 0) & (dlt < WINDOW) & (w_pos[None, :] >= 0)
    p3 = masked_softmax(s3, mw[None, :, None, None, :])
    o_w = jnp.einsum('bqgrk,bkgd->bqgrd', p3.astype(wv.dtype), wv)
    return gates[..., 0:1] * o_c + gates[..., 1:2] * o_s + gates[..., 2:3] * o_w


NSA_KT = 1024
_NT_DIMS = (((1,), (1,)), ((), ()))


def _softmax_tile_update(s, mask, v, sl, m_scr, l_scr, acc_scr):
    m_old = m_scr[sl]
    m_new = jnp.maximum(m_old, jnp.max(jnp.where(mask, s, NEG_BIG), axis=-1, keepdims=True))
    alpha = jnp.exp(m_old - m_new)
    e = jnp.where(mask, jnp.exp(s - m_new), 0.0)
    l_scr[sl] = alpha * l_scr[sl] + jnp.sum(e, axis=-1, keepdims=True)
    acc_scr[sl] = alpha * acc_scr[sl] + jnp.dot(e.astype(MXU_DTYPE), v, preferred_element_type=jnp.float32)
    m_scr[sl] = m_new


def _nsa_prompt_kernel(qn_ref, qr_ref, gate_ref, kc_ref, vc_ref, ovl_ref, eexp_ref,
                       ks_ref, vs_ref, kw_ref, vw_ref, o_ref, m_scr, l_scr, acc_scr):
    qi = pl.program_id(2)
    q0 = qi * Q_BLOCK
    rows = Q_PER_KV * Q_BLOCK
    scale = HEAD_DIM ** -0.5
    qn = (qn_ref[0, 0].reshape(rows, HEAD_DIM) * scale).astype(MXU_DTYPE)
    qr = (qr_ref[0, 0].reshape(rows, HEAD_DIM) * scale).astype(MXU_DTYPE)
    pos_row = q0 + (lax.broadcasted_iota(jnp.int32, (rows, 1), 0) & (Q_BLOCK - 1))

    kc = kc_ref[0, 0]
    ncp = kc.shape[0]
    s = lax.dot_general(qn, kc, _NT_DIMS, preferred_element_type=jnp.float32)
    c_end = lax.broadcasted_iota(jnp.int32, (rows, ncp), 1) * CMP_STRIDE + (CMP_BLOCK - 1)
    cmask = c_end <= pos_row
    m = jnp.max(jnp.where(cmask, s, NEG_BIG), axis=-1, keepdims=True)
    e = jnp.where(cmask, jnp.exp(s - m), 0.0)
    p = e / jnp.maximum(jnp.sum(e, axis=-1, keepdims=True), 1e-30)
    o_c = jnp.dot(p.astype(MXU_DTYPE), vc_ref[0, 0], preferred_element_type=jnp.float32)

    psum = p[0:Q_BLOCK]
    for r in range(1, Q_PER_KV):
        psum = psum + p[r * Q_BLOCK:(r + 1) * Q_BLOCK]
    p_hi = psum.astype(MXU_DTYPE)
    p_lo = (psum - p_hi.astype(jnp.float32)).astype(MXU_DTYPE)
    ovl = ovl_ref[...]
    imp = (jnp.dot(p_hi, ovl, preferred_element_type=jnp.float32)
           + jnp.dot(p_lo, ovl, preferred_element_type=jnp.float32))
    ns = imp.shape[1]
    j = lax.broadcasted_iota(jnp.int32, (Q_BLOCK, ns), 1)
    cur = (q0 + lax.broadcasted_iota(jnp.int32, (Q_BLOCK, 1), 0)) // SLC_BLOCK
    forced = (j == 0) | (j == cur) | (j == cur - 1)
    adj = jnp.where(j <= cur, jnp.where(forced, FORCE_SCORE, imp), -FORCE_SCORE)
    rank = jnp.zeros((Q_BLOCK, ns), jnp.int32)
    for k in range(ns):
        col = adj[:, k:k + 1]
        beats = (col > adj) | ((col == adj) & (j > k))
        rank = rank + beats.astype(jnp.int32)
    sel = (rank < N_SELECT).astype(MXU_DTYPE)

    def init():
        m_scr[...] = jnp.full(m_scr.shape, NEG_BIG, jnp.float32)
        l_scr[...] = jnp.zeros(l_scr.shape, jnp.float32)
        acc_scr[...] = jnp.zeros(acc_scr.shape, jnp.float32)

    def finish():
        return acc_scr[...] / jnp.maximum(l_scr[...], 1e-30)

    init()
    pos_q = pos_row[0:Q_BLOCK]
    all_rows = slice(0, rows)

    def slc_body(kt, carry):
        k0 = pl.multiple_of(kt * NSA_KT, NSA_KT)
        s2 = lax.dot_general(qr, ks_ref[0, 0, pl.ds(k0, NSA_KT), :], _NT_DIMS,
                             preferred_element_type=jnp.float32)
        bm = jnp.dot(sel, eexp_ref[kt], preferred_element_type=jnp.float32)
        kpos = k0 + lax.broadcasted_iota(jnp.int32, (Q_BLOCK, NSA_KT), 1)
        mask = (bm > 0.5) & (kpos <= pos_q)
        mask = jnp.concatenate([mask] * Q_PER_KV, axis=0)
        _softmax_tile_update(s2, mask, vs_ref[0, 0, pl.ds(k0, NSA_KT), :], all_rows, m_scr, l_scr, acc_scr)
        return carry

    lax.fori_loop(0, (q0 + Q_BLOCK + NSA_KT - 1) // NSA_KT, slc_body, 0)
    o_s = finish()

    init()
    w0 = pl.multiple_of(jnp.maximum(q0 - WINDOW, 0), Q_BLOCK)
    span = WINDOW + Q_BLOCK
    s3 = lax.dot_general(qr, kw_ref[0, 0, pl.ds(w0, span), :], _NT_DIMS,
                         preferred_element_type=jnp.float32)
    dlt = pos_q - (w0 + lax.broadcasted_iota(jnp.int32, (Q_BLOCK, span), 1))
    wmask = (dlt >= 0) & (dlt < WINDOW)
    wmask = jnp.concatenate([wmask] * Q_PER_KV, axis=0)
    _softmax_tile_update(s3, wmask, vw_ref[0, 0, pl.ds(w0, span), :], all_rows, m_scr, l_scr, acc_scr)
    o_w = finish()

    gate = jax.nn.sigmoid(gate_ref[0, 0])
    outs = []
    for r in range(Q_PER_KV):
        sl = slice(r * Q_BLOCK, (r + 1) * Q_BLOCK)
        outs.append(gate[:, 3 * r:3 * r + 1] * o_c[sl] + gate[:, 3 * r + 1:3 * r + 2] * o_s[sl]
                    + gate[:, 3 * r + 2:3 * r + 3] * o_w[sl])
    o_ref[0] = jnp.concatenate(outs, axis=1)


def nsa_prompt(q_r, q_n, zg, kc, vc, slc_kv, win_kv):
    b, t = q_r.shape[:2]
    assert t % NSA_KT == 0 and t >= WINDOW + Q_BLOCK and WINDOW % Q_BLOCK == 0
    nb = kc.shape[1]
    ncp = -(-nb // 128) * 128
    ns = t // SLC_BLOCK
    heads_first = lambda a: a.transpose(0, 2, 3, 1, 4)
    kv_first = lambda a: a.transpose(0, 2, 1, 3).astype(MXU_DTYPE)
    pad_c = lambda a: jnp.pad(a, ((0, 0), (0, ncp - nb), (0, 0), (0, 0)))
    gate_in = zg.reshape(b, t, KV_HEADS, 3 * Q_PER_KV).transpose(0, 2, 1, 3)
    ovl = np.zeros((ncp, ns), np.float32)
    ovl[:nb] = np.asarray(overlap_matrix_np(nb, ns))
    key_blk = (np.arange(t) // SLC_BLOCK).reshape(t // NSA_KT, 1, NSA_KT)
    eexp = (key_blk == np.arange(ns).reshape(1, ns, 1)).astype(np.float32)
    rows = Q_PER_KV * Q_BLOCK
    q_spec = pl.BlockSpec((1, 1, Q_PER_KV, Q_BLOCK, HEAD_DIM), lambda bi, g, qi: (bi, g, 0, qi, 0))
    c_spec = pl.BlockSpec((1, 1, ncp, HEAD_DIM), lambda bi, g, qi: (bi, g, 0, 0))
    kv_spec = pl.BlockSpec((1, 1, t, HEAD_DIM), lambda bi, g, qi: (bi, g, 0, 0))
    return pl.pallas_call(
        _nsa_prompt_kernel,
        out_shape=jax.ShapeDtypeStruct((b, t, BRANCH_WIDTH), jnp.float32),
        grid=(b, KV_HEADS, t // Q_BLOCK),
        in_specs=[q_spec, q_spec,
                  pl.BlockSpec((1, 1, Q_BLOCK, 3 * Q_PER_KV), lambda bi, g, qi: (bi, g, qi, 0)),
                  c_spec, c_spec,
                  pl.BlockSpec((ncp, ns), lambda bi, g, qi: (0, 0)),
                  pl.BlockSpec((t // NSA_KT, ns, NSA_KT), lambda bi, g, qi: (0, 0, 0)),
                  kv_spec, kv_spec, kv_spec, kv_spec],
        out_specs=pl.BlockSpec((1, Q_BLOCK, Q_PER_KV * HEAD_DIM), lambda bi, g, qi: (bi, qi, g)),
        scratch_shapes=[pltpu.VMEM((rows, 1), jnp.float32),
                        pltpu.VMEM((rows, 1), jnp.float32),
                        pltpu.VMEM((rows, HEAD_DIM), jnp.float32)],
        compiler_params=pltpu.CompilerParams(
            dimension_semantics=("parallel", "parallel", "arbitrary"),
            vmem_limit_bytes=VMEM_LIMIT_BYTES),
        name="nsa_prompt",
    )(heads_first(q_n), heads_first(q_r), gate_in,
      kv_first(pad_c(kc)), kv_first(pad_c(vc)),
      jnp.asarray(ovl, MXU_DTYPE), jnp.asarray(eexp, MXU_DTYPE),
      kv_first(slc_kv[:, :, 0]), kv_first(slc_kv[:, :, 1]),
      kv_first(win_kv[:, :, 0]), kv_first(win_kv[:, :, 1]))


SEGS_PER_PAGE = PAGE_SIZE // CMP_STRIDE
SEG_WIDTH = CMP_STRIDE * HEAD_DIM
GD = KV_HEADS * HEAD_DIM


def _masked_softmax_parts(s, mask, s_new, new_on):
    m = jnp.max(jnp.where(mask, s, NEG_BIG), axis=-1, keepdims=True)
    m = jnp.maximum(m, jnp.where(new_on, s_new, NEG_BIG))
    e = jnp.where(mask, jnp.exp(s - m), 0.0)
    e_new = jnp.where(new_on, jnp.exp(s_new - m), 0.0)
    den = jnp.maximum(jnp.sum(e, axis=-1, keepdims=True) + e_new, 1e-30)
    return e, e_new, den


def _nsa_sample_kernel(pt_ref, qn_ref, qr_ref, gate_ref, new_ref, w1_ref, b0_ref, w2_ref, kn_ref,
                       ovl_ref, eexp_ref, win_ref, *rest, n_pages, past_len, win_len):
    cmp_refs = rest[:n_pages]
    slc_refs = rest[n_pages:2 * n_pages]
    o_ref = rest[2 * n_pages]
    nh = N_HEADS
    n_seg = n_pages * SEGS_PER_PAGE
    row_group = lax.broadcasted_iota(jnp.int32, (nh, 1), 0) // Q_PER_KV

    outs = []
    for c in range(2):
        x = jnp.concatenate(
            [jnp.concatenate([cmp_refs[p][0, 0, c, g] for p in range(n_pages)], axis=0)
             for g in range(KV_HEADS)], axis=0).astype(MXU_DTYPE)
        ab = jnp.dot(x, w1_ref[c], preferred_element_type=jnp.float32)
        a = ab[:, :HEAD_DIM]
        b_next = pltpu.roll(ab[:, HEAD_DIM:], KV_HEADS * n_seg - 1, axis=0)
        hid = jax.nn.gelu(a + b_next + b0_ref[c])
        outs.append(jnp.dot(hid.astype(MXU_DTYPE), w2_ref[c].astype(MXU_DTYPE),
                            preferred_element_type=jnp.float32))
    kc_all = outs[0]
    kc_all = kc_all * lax.rsqrt(jnp.mean(kc_all * kc_all, axis=-1, keepdims=True) + NORM_EPS) * kn_ref[...]
    vc_all = outs[1]

    qn = qn_ref[0].astype(MXU_DTYPE)
    s_c = jnp.zeros((nh, n_seg), jnp.float32)
    for g in range(KV_HEADS):
        kc_g = kc_all[g * n_seg:(g + 1) * n_seg].astype(MXU_DTYPE)
        s_g = lax.dot_general(qn, kc_g, _NT_DIMS, preferred_element_type=jnp.float32)
        s_c = jnp.where(row_group == g, s_g, s_c)
    blk = lax.broadcasted_iota(jnp.int32, (nh, n_seg), 1)
    cmask = (blk * CMP_STRIDE + (CMP_BLOCK - 1) <= past_len) & (blk < n_seg - CMP_RATIO + 1)
    m = jnp.max(jnp.where(cmask, s_c, NEG_BIG), axis=-1, keepdims=True)
    e = jnp.where(cmask, jnp.exp(s_c - m), 0.0)
    p = e / jnp.maximum(jnp.sum(e, axis=-1, keepdims=True), 1e-30)
    o_c = jnp.zeros((nh, HEAD_DIM), jnp.float32)
    p_bf = p.astype(MXU_DTYPE)
    psum = jnp.zeros((nh, n_seg), jnp.float32)
    for g in range(KV_HEADS):
        vc_g = vc_all[g * n_seg:(g + 1) * n_seg].astype(MXU_DTYPE)
        in_g = row_group == g
        o_c = jnp.where(in_g, jnp.dot(p_bf, vc_g, preferred_element_type=jnp.float32), o_c)
        psum = jnp.where(in_g, jnp.sum(jnp.where(in_g, p, 0.0), axis=0, keepdims=True), psum)

    p_hi = psum.astype(MXU_DTYPE)
    p_lo = (psum - p_hi.astype(jnp.float32)).astype(MXU_DTYPE)
    ovl = ovl_ref[...]
    imp = (jnp.dot(p_hi, ovl, preferred_element_type=jnp.float32)
           + jnp.dot(p_lo, ovl, preferred_element_type=jnp.float32))
    ns_pad = imp.shape[1]
    ns = past_len // SLC_BLOCK + 1
    cur = past_len // SLC_BLOCK
    j = lax.broadcasted_iota(jnp.int32, (nh, ns_pad), 1)
    forced = (j == 0) | (j == cur) | (j == cur - 1)
    adj = jnp.where(j <= cur, jnp.where(forced, FORCE_SCORE, imp), -FORCE_SCORE)
    adj = jnp.where(j < ns, adj, PEER_NEG)
    rank = jnp.zeros((nh, ns_pad), jnp.int32)
    for k in range(ns):
        col = adj[:, k:k + 1]
        beats = (col > adj) | ((col == adj) & (j > k))
        rank = rank + beats.astype(jnp.int32)
    sel = (rank < N_SELECT) & (j < ns)
    tok_mask = jnp.dot(sel.astype(MXU_DTYPE), eexp_ref[...], preferred_element_type=jnp.float32) > 0.5
    new_sel = jnp.sum(jnp.where(sel & (j == cur), 1.0, 0.0), axis=-1, keepdims=True) > 0.5

    qr = qr_ref[0]
    qr_bf = qr.astype(MXU_DTYPE)
    new = new_ref[0]
    to_mxu = lambda a: a.astype(MXU_DTYPE).astype(jnp.float32)
    s_new = jnp.sum(to_mxu(qr) * to_mxu(new[0:1]), axis=-1, keepdims=True)
    s_sel = jnp.concatenate(
        [jnp.dot(qr_bf, slc_refs[pg][0, 0, 0:GD, :].astype(MXU_DTYPE), preferred_element_type=jnp.float32)
         for pg in range(n_pages)], axis=1)
    e, e_new, den = _masked_softmax_parts(s_sel, tok_mask, s_new, new_sel)
    e_bf = e.astype(MXU_DTYPE)
    o_s = to_mxu(e_new) * to_mxu(new[1:2])
    for pg in range(n_pages):
        o_s = o_s + lax.dot_general(e_bf[:, pg * PAGE_SIZE:(pg + 1) * PAGE_SIZE],
                                    slc_refs[pg][0, 0, GD:2 * GD, :].astype(MXU_DTYPE), _NT_DIMS,
                                    preferred_element_type=jnp.float32)
    o_s = o_s / den

    sw_new = jnp.sum(to_mxu(qr) * to_mxu(new[2:3]), axis=-1, keepdims=True)
    s_w = jnp.dot(qr_bf, win_ref[0, 0, 0:GD, :].astype(MXU_DTYPE), preferred_element_type=jnp.float32)
    dlt = win_len - lax.broadcasted_iota(jnp.int32, (nh, win_len), 1)
    wmask = (dlt >= 0) & (dlt < WINDOW) & (past_len - dlt >= 0)
    e, e_new, den = _masked_softmax_parts(s_w, wmask, sw_new, jnp.full((nh, 1), True))
    o_w = (to_mxu(e_new) * to_mxu(new[3:4])
           + lax.dot_general(e.astype(MXU_DTYPE), win_ref[0, 0, GD:2 * GD, :].astype(MXU_DTYPE), _NT_DIMS,
                             preferred_element_type=jnp.float32)) / den

    def own_block(o_bd):
        out = jnp.zeros((nh, HEAD_DIM), jnp.float32)
        for g in range(KV_HEADS):
            out = jnp.where(row_group == g, o_bd[:, g * HEAD_DIM:(g + 1) * HEAD_DIM], out)
        return out

    gate = jax.nn.sigmoid(gate_ref[0])
    o_ref[0] = gate[:, 0:1] * o_c + gate[:, 1:2] * own_block(o_s) + gate[:, 2:3] * own_block(o_w)


def nsa_sample(q_r, q_n, zg, skv_new, wkv_new, cmp_pages, slc_pages, win_t, page_table, layer,
               cmp_pe, cmp_w1, cmp_b1, cmp_w2, kn):
    bs = q_r.shape[0]
    n_pages = page_table.shape[1]
    past_len = n_pages * PAGE_SIZE
    win_len = win_t.shape[-1]
    n_seg = n_pages * SEGS_PER_PAGE
    nb = n_seg - CMP_RATIO + 1
    ns = past_len // SLC_BLOCK + 1
    ns_pad = -(-ns // 128) * 128
    scale = HEAD_DIM ** -0.5
    qn = q_n.reshape(bs, N_HEADS, HEAD_DIM) * scale
    eye = jnp.eye(KV_HEADS, dtype=q_r.dtype)
    qr_bd = (q_r.reshape(bs, KV_HEADS, Q_PER_KV, 1, HEAD_DIM) * scale
             * eye[None, :, None, :, None]).reshape(bs, N_HEADS, GD)
    gate_in = zg.reshape(bs, N_HEADS, 3)
    new = jnp.concatenate([skv_new.reshape(bs, 2, GD), wkv_new.reshape(bs, 2, GD)], axis=1)
    w1r = cmp_w1.reshape(2, CMP_RATIO, SEG_WIDTH, HEAD_DIM)
    w1c = jnp.concatenate([w1r[:, r] for r in range(CMP_RATIO)], axis=-1).astype(MXU_DTYPE)
    b0 = (jnp.einsum('cld,cldh->ch', cmp_pe, cmp_w1, precision=lax.Precision.HIGHEST) + cmp_b1)[:, None, :]
    ovl = np.zeros((n_seg, ns_pad), np.float32)
    ovl[:nb, :ns] = overlap_matrix_np(nb, ns)
    eexp = (np.arange(past_len)[None, :] // SLC_BLOCK == np.arange(ns_pad)[:, None]).astype(np.float32)
    const = lambda shape: pl.BlockSpec(shape, lambda b, pt: (0,) * len(shape))
    per_b = lambda shape: pl.BlockSpec((1,) + shape, lambda b, pt: (b,) + (0,) * len(shape))
    cmp_spec = [pl.BlockSpec((1, 1, 2, KV_HEADS, SEGS_PER_PAGE, SEG_WIDTH),
                             functools.partial(lambda b, pt, pg: (layer, pt[b, pg], 0, 0, 0, 0), pg=pg))
                for pg in range(n_pages)]
    slc_spec = [pl.BlockSpec((1, 1, 2 * GD, PAGE_SIZE),
                             functools.partial(lambda b, pt, pg: (layer, pt[b, pg], 0, 0), pg=pg))
                for pg in range(n_pages)]
    grid_spec = pltpu.PrefetchScalarGridSpec(
        num_scalar_prefetch=1,
        grid=(bs,),
        in_specs=[per_b((N_HEADS, HEAD_DIM)), per_b((N_HEADS, GD)), per_b((N_HEADS, 3)), per_b((4, GD)),
                  const((2, SEG_WIDTH, CMP_RATIO * HEAD_DIM)), const((2, 1, HEAD_DIM)),
                  const((2, HEAD_DIM, HEAD_DIM)), const((1, HEAD_DIM)),
                  const((n_seg, ns_pad)), const((ns_pad, past_len)),
                  pl.BlockSpec((1, 1, 2 * GD, win_len), lambda b, pt: (layer, b, 0, 0))]
                 + cmp_spec + slc_spec,
        out_specs=per_b((N_HEADS, HEAD_DIM)),
    )
    out = pl.pallas_call(
        functools.partial(_nsa_sample_kernel, n_pages=n_pages, past_len=past_len, win_len=win_len),
        out_shape=jax.ShapeDtypeStruct((bs, N_HEADS, HEAD_DIM), jnp.float32),
        grid_spec=grid_spec,
        compiler_params=pltpu.CompilerParams(
            dimension_semantics=("arbitrary",),
            vmem_limit_bytes=VMEM_LIMIT_BYTES),
        name="nsa_sample",
    )(page_table, qn, qr_bd, gate_in, new, w1c, b0, cmp_w2, kn.reshape(1, HEAD_DIM),
      jnp.asarray(ovl, MXU_DTYPE), jnp.asarray(eexp, MXU_DTYPE), win_t,
      *([cmp_pages] * n_pages), *([slc_pages] * n_pages))
    return out.reshape(bs, 1, BRANCH_WIDTH)


def gmlp_mix(u, v, w_s, b_s):
    b, t, _ = v.shape
    nc = -(-t // CHUNK)
    vp = jnp.pad(v, ((0, 0), (0, nc * CHUNK - t), (0, 0))).reshape(b, nc, CHUNK, GMLP_GROUPS, GMLP_WIDTH // GMLP_GROUPS)
    w = w_s * jnp.tril(jnp.ones((CHUNK, CHUNK), w_s.dtype))
    mixed = jnp.einsum('gpq,bcqgd->bcpgd', w, vp) + b_s.T[None, None, :, :, None]
    return u * mixed.reshape(b, nc * CHUNK, GMLP_WIDTH)[:, :t]


def conformer_conv(zglu, buf, w_dw, b_dw, norm_g):
    a, gt = jnp.split(zglu, 2, axis=-1)
    glu = a * jax.nn.sigmoid(gt)
    xin = jnp.concatenate([buf.astype(glu.dtype), glu], axis=1)
    y = lax.conv_general_dilated(xin, w_dw[:, None, :].astype(glu.dtype), (1,), 'VALID',
                                 dimension_numbers=('NWC', 'WIO', 'NWC'), feature_group_count=CONV_WIDTH) + b_dw
    y = jax.nn.silu(rms_norm(y, norm_g))
    return y, xin[:, xin.shape[1] - (CONV_K - 1):]


def mixer_merge(x, h, o_nsa, o_gmlp, o_conv, gate, w_branch, w_merge, b_merge, w_out):
    b, t, d = x.shape
    mg = jax.nn.sigmoid(mm3(h, w_merge) + b_merge).reshape(b, t, 3, d)
    y = (mg[:, :, 0] * mm3(o_nsa, w_branch[0]) + mg[:, :, 1] * mm3(o_gmlp, w_branch[1])
         + mg[:, :, 2] * mm3(o_conv, w_branch[2]))
    return x + gate[:, None, :] * mm3(y, w_out)


PEER_TB = 512
PEER_I1 = 8
PEER_SPLIT = 2
PEER_NEG = -3.0e38
PEER_NRANK = PEER_TOPK + 1
PEER_VROWS = 24
PEER_NCAND = PEER_VROWS + 7 * 8 + 16


def _extract_top(cur_ref, out_ref, n_out):
    n = cur_ref.shape[0]
    iota = lax.broadcasted_iota(jnp.int32, cur_ref.shape, 0)
    for k in range(n_out):
        cur = cur_ref[...]
        m = jnp.max(cur, axis=0, keepdims=True)
        first = jnp.min(jnp.where(cur == m, iota, n), axis=0, keepdims=True)
        cur_ref[...] = jnp.where(iota == first, PEER_NEG, cur)
        out_ref[k:k + 1, :] = m


def _peer_route_kernel(x_ref, w_ref, k_ref, thr1_ref, e1_ref, s2_ref, e2_ref,
                       work_ref, v1_ref, v2_ref, cand_ref, ctop_ref):
    x = x_ref[...].astype(MXU_DTYPE)
    w = w_ref[...].astype(MXU_DTYPE)
    q = jnp.dot(x, w, preferred_element_type=jnp.float32)
    half = PEER_QDIM // 2
    nt = (((1,), (1,)), ((), ()))
    s = []
    for c in range(2):
        qc = q[:, c * half:(c + 1) * half].astype(MXU_DTYPE)
        kc = k_ref[0, c].astype(MXU_DTYPE)
        s.append(lax.dot_general(kc, qc, nt, preferred_element_type=jnp.float32))
    s1, s2 = s
    v1_ref[...] = jnp.full(v1_ref.shape, PEER_NEG, jnp.float32)
    v2_ref[...] = jnp.full(v2_ref.shape, PEER_NEG, jnp.float32)
    work_ref[...] = s1
    _extract_top(work_ref, v1_ref, PEER_NRANK)
    work_ref[...] = s2
    _extract_top(work_ref, v2_ref, PEER_NRANK)
    v1 = v1_ref[...]
    v2 = v2_ref[...]
    nv = PEER_VROWS
    cand_ref[0:nv, :] = v1[0:1] + v2
    for a in range(1, 8):
        cand_ref[nv + 8 * (a - 1):nv + 8 * a, :] = v1[a:a + 1] + v2[0:8]
    cand_ref[nv + 56:nv + 72, :] = v1[8:nv] + v2[0:1]
    _extract_top(cand_ref, ctop_ref, PEER_NRANK)
    tau = 0.5 * (ctop_ref[PEER_TOPK - 1:PEER_TOPK, :] + ctop_ref[PEER_TOPK:PEER_TOPK + 1, :])
    m1 = v1[0:1]
    m2 = v2[0:1]
    e1top = jnp.exp(v1 - m1)
    e2top = jnp.exp(v2 - m2)
    z = jnp.zeros_like(tau)
    for a in range(PEER_NRANK):
        sel = v2 >= (tau - v1[a:a + 1])
        z = z + jnp.sum(jnp.where(sel, e2top, 0.0), axis=0, keepdims=True) * e1top[a:a + 1]
    inv_z = 1.0 / z
    thr1_ref[0] = tau - s1
    e1_ref[0] = jnp.exp(s1 - m1) * inv_z
    s2_ref[0] = s2
    e2_ref[0] = jnp.exp(s2 - m2)


def peer_route(xm, w_pq, sub_keys):
    t, d = xm.shape
    tb = PEER_TB
    out = jax.ShapeDtypeStruct((PEER_HEADS, PEER_KEYS, t), jnp.float32)
    ospec = pl.BlockSpec((1, PEER_KEYS, tb), lambda i, h: (h, 0, i))
    return pl.pallas_call(
        _peer_route_kernel,
        out_shape=(out, out, out, out),
        grid=(t // tb, PEER_HEADS),
        in_specs=[pl.BlockSpec((tb, d), lambda i, h: (i, 0)),
                  pl.BlockSpec((d, PEER_QDIM), lambda i, h: (0, h)),
                  pl.BlockSpec((1, 2, PEER_KEYS, PEER_QDIM // 2), lambda i, h: (h, 0, 0, 0))],
        out_specs=(ospec, ospec, ospec, ospec),
        scratch_shapes=[pltpu.VMEM((PEER_KEYS, tb), jnp.float32),
                        pltpu.VMEM((PEER_VROWS, tb), jnp.float32),
                        pltpu.VMEM((PEER_VROWS, tb), jnp.float32),
                        pltpu.VMEM((PEER_NCAND, tb), jnp.float32),
                        pltpu.VMEM((PEER_VROWS, tb), jnp.float32)],
        compiler_params=pltpu.CompilerParams(
            dimension_semantics=("parallel", "arbitrary"),
            vmem_limit_bytes=VMEM_LIMIT_BYTES),
        name="peer_route",
    )(xm, w_pq, sub_keys)


def _peer_dense_kernel(xt_ref, u_ref, vt_ref, thr1_ref, e1_ref, s2_ref, e2_ref, o_ref, s_scr, w_scr):
    @pl.when(pl.program_id(1) == 0)
    def _():
        o_ref[...] = jnp.zeros_like(o_ref)

    tb = o_ref.shape[1]
    nk = PEER_KEYS
    rows_per = PEER_I1 // PEER_SPLIT
    sub = rows_per * nk
    for part in range(PEER_SPLIT):
        s_scr[part] = jnp.dot(u_ref[part * sub:(part + 1) * sub, :], xt_ref[...],
                              preferred_element_type=jnp.float32)
    acc = None
    for part in range(PEER_SPLIT):
        for jj in range(rows_per):
            j = part * rows_per + jj
            for t0 in range(0, tb, 128):
                g = jnp.zeros((nk, 128), jnp.float32)
                for h in range(PEER_HEADS):
                    thr = thr1_ref[h, j:j + 1, t0:t0 + 128]
                    e1 = e1_ref[h, j:j + 1, t0:t0 + 128]
                    sel = s2_ref[h, :, t0:t0 + 128] >= thr
                    g = g + jnp.where(sel, e2_ref[h, :, t0:t0 + 128], 0.0) * e1
                act = jax.nn.gelu(s_scr[part, jj * nk:(jj + 1) * nk, t0:t0 + 128])
                w_scr[part, jj * nk:(jj + 1) * nk, t0:t0 + 128] = (act * g).astype(MXU_DTYPE)
        contrib = jnp.dot(vt_ref[:, part * sub:(part + 1) * sub], w_scr[part],
                          preferred_element_type=jnp.float32)
        acc = contrib if acc is None else acc + contrib
    o_ref[...] += acc


def peer_dense(xt, u_bf, vt_bf, thr1, e1, s2, e2):
    d, t = xt.shape
    n_exp = u_bf.shape[0]
    tb = PEER_TB
    ec = PEER_I1 * PEER_KEYS
    row_spec = pl.BlockSpec((PEER_HEADS, PEER_I1, tb), lambda i, c: (0, c, i))
    full_spec = pl.BlockSpec((PEER_HEADS, PEER_KEYS, tb), lambda i, c: (0, 0, i))
    return pl.pallas_call(
        _peer_dense_kernel,
        out_shape=jax.ShapeDtypeStruct((d, t), jnp.float32),
        grid=(t // tb, n_exp // ec),
        in_specs=[pl.BlockSpec((d, tb), lambda i, c: (0, i)),
                  pl.BlockSpec((ec, d), lambda i, c: (c, 0)),
                  pl.BlockSpec((d, ec), lambda i, c: (0, c)),
                  row_spec, row_spec, full_spec, full_spec],
        out_specs=pl.BlockSpec((d, tb), lambda i, c: (0, i)),
        scratch_shapes=[pltpu.VMEM((PEER_SPLIT, ec // PEER_SPLIT, tb), jnp.float32),
                        pltpu.VMEM((PEER_SPLIT, ec // PEER_SPLIT, tb), MXU_DTYPE)],
        compiler_params=pltpu.CompilerParams(
            dimension_semantics=("parallel", "arbitrary"),
            vmem_limit_bytes=VMEM_LIMIT_BYTES),
        name="peer_dense",
    )(xt, u_bf, vt_bf, thr1, e1, s2, e2)


def peer_ffn(xm, w_pq, sub_keys, u_bf, vt_bf):
    t, d = xm.shape
    tp = -(-t // PEER_TB) * PEER_TB
    xm_p = jnp.pad(xm, ((0, tp - t), (0, 0)))
    thr1, e1, s2, e2 = peer_route(xm_p, w_pq, sub_keys)
    out_t = peer_dense(xm_p.T.astype(MXU_DTYPE), u_bf, vt_bf, thr1, e1, s2, e2)
    return out_t.T[:t]


def kernel(x_prompt, x_sample, cache_cmp_kv, cache_slc_kv, state_win_kv, state_conv, page_table,
           c_prompt, c_sample, w_ada, b_ada, norm_mix, norm_ffn, w_in, q_norm, k_norm,
           cmp_pe, cmp_w1, cmp_b1, cmp_w2, gmlp_norm, gmlp_ws, gmlp_bs, conv_w, conv_b, conv_norm,
           w_branch, w_merge, b_merge, w_out, peer_wq, peer_keys, peer_u, peer_v):
    bp, t = x_prompt.shape[:2]
    bs, ds = x_sample.shape[:2]
    depth = w_in.shape[0]
    past_len = page_table.shape[1] * PAGE_SIZE
    pos_p = jnp.arange(t, dtype=jnp.int32)
    pos_s = past_len + jnp.arange(ds, dtype=jnp.int32)
    xp, xs = x_prompt, x_sample
    assert ds == 1
    n_phys = cache_cmp_kv.shape[1]
    cmp_pages = cache_cmp_kv.reshape(depth, n_phys, SEGS_PER_PAGE, CMP_STRIDE, 2, KV_HEADS, HEAD_DIM)
    cmp_pages = cmp_pages.transpose(0, 1, 4, 5, 2, 3, 6).reshape(
        depth, n_phys, 2, KV_HEADS, SEGS_PER_PAGE, SEG_WIDTH)
    slc_pages = cache_slc_kv.transpose(0, 1, 3, 4, 5, 2).reshape(depth, n_phys, 2 * GD, PAGE_SIZE)
    win_t = state_win_kv.transpose(0, 1, 3, 4, 5, 2).reshape(depth, bs, 2 * GD, state_win_kv.shape[2])
    cmp_p, cmp_s, slc_p, slc_s, win_p, win_s, conv_p, conv_s, gv_s = [], [], [], [], [], [], [], [], []
    for l in range(depth):
        mp, msm = adaln(c_prompt, c_sample, w_ada[l], b_ada[l])
        w_in_l, w_merge_l, w_branch_l, w_out_l = (w.astype(MXU_DTYPE) for w in
                                                  (w_in[l], w_merge[l], w_branch[l], w_out[l]))
        hp, q_r, q_n, gates, zg, ckv, skv, wkv, u, v, zglu = mixer_front(
            xp, mp[0], mp[1], norm_mix[l], w_in_l, q_norm[l], k_norm[l], gmlp_norm[l], pos_p)
        kc, vc, c_end = compress(ckv, cmp_pe[l], cmp_w1[l], cmp_b1[l], cmp_w2[l], k_norm[l, 0])
        o_nsa = nsa_prompt(q_r, q_n, zg, kc, vc, skv, wkv)
        o_g = gmlp_mix(u, v, gmlp_ws[l], gmlp_bs[l])
        o_c, buf = conformer_conv(zglu, jnp.zeros((bp, CONV_K - 1, CONV_WIDTH), zglu.dtype),
                                  conv_w[l], conv_b[l], conv_norm[l])
        xp = mixer_merge(xp, hp, o_nsa, o_g, o_c, mp[2], w_branch_l, w_merge_l, b_merge[l], w_out_l)
        cmp_p.append(ckv)
        slc_p.append(skv)
        win_p.append(wkv[:, t - min(WINDOW, t):])
        conv_p.append(buf)
        hs, q_r, q_n, gates, zg, ckv, skv, wkv, u, v, zglu = mixer_front(
            xs, msm[0], msm[1], norm_mix[l], w_in_l, q_norm[l], k_norm[l], gmlp_norm[l], pos_s)
        win_full = jnp.concatenate([state_win_kv[l], wkv], axis=1)
        o_nsa = nsa_sample(q_r, q_n, zg, skv, wkv, cmp_pages, slc_pages, win_t, page_table, l,
                           cmp_pe[l], cmp_w1[l], cmp_b1[l], cmp_w2[l], k_norm[l, 0])
        o_g = gmlp_mix(u, v, gmlp_ws[l], gmlp_bs[l])
        o_c, buf = conformer_conv(zglu, state_conv[l], conv_w[l], conv_b[l], conv_norm[l])
        xs = mixer_merge(xs, hs, o_nsa, o_g, o_c, msm[2], w_branch_l, w_merge_l, b_merge[l], w_out_l)
        d = xp.shape[-1]
        xm_all = jnp.concatenate([modulate(xp, norm_ffn[l], mp[3], mp[4]).reshape(bp * t, d),
                                  modulate(xs, norm_ffn[l], msm[3], msm[4]).reshape(bs * ds, d)], axis=0)
        ffn = peer_ffn(xm_all, peer_wq[l].astype(MXU_DTYPE), peer_keys[l],
                       peer_u[l].astype(MXU_DTYPE), peer_v[l].T.astype(MXU_DTYPE))
        xp = xp + mp[5][:, None, :] * ffn[:bp * t].reshape(bp, t, d)
        xs = xs + msm[5][:, None, :] * ffn[bp * t:].reshape(bs, ds, d)
        cmp_s.append(ckv)
        slc_s.append(skv)
        lw = win_full.shape[1]
        win_s.append(win_full[:, lw - min(WINDOW, lw):])
        conv_s.append(buf)
        gv_s.append(v)
    return (xp, xs, jnp.stack(cmp_p), jnp.stack(cmp_s), jnp.stack(slc_p), jnp.stack(slc_s),
            jnp.stack(win_p), jnp.stack(win_s), jnp.stack(conv_p), jnp.stack(conv_s), jnp.stack(gv_s))
```

```python
import functools

import jax
import jax.numpy as jnp
import numpy as np
from jax import lax
from jax.experimental import pallas as pl
from jax.experimental.pallas import tpu as pltpu

D_MODEL = 2048
PAGE_SIZE = 128
BRANCH_WIDTH = D_MODEL // 2
HEAD_DIM = 64
N_HEADS = BRANCH_WIDTH // HEAD_DIM
KV_HEADS = 4
Q_PER_KV = N_HEADS // KV_HEADS
KV_WIDTH = KV_HEADS * HEAD_DIM
ROT_DIM = HEAD_DIM // 4
ROPE_THETA = 500000.0
CMP_BLOCK = 32
CMP_STRIDE = 16
CMP_RATIO = CMP_BLOCK // CMP_STRIDE
SLC_BLOCK = 64
N_SELECT = 16
WINDOW = 512
Q_BLOCK = 128
GMLP_WIDTH = BRANCH_WIDTH
GMLP_GROUPS = 8
CHUNK = 128
CONV_WIDTH = BRANCH_WIDTH
CONV_K = 31
PEER_HEADS = 8
PEER_KEYS = 128
PEER_QDIM = 256
PEER_TOPK = 16
PEER_BLOCK = 128
NORM_EPS = 1e-6
NEG_BIG = -1e30
FORCE_SCORE = 1e9

VMEM_LIMIT_BYTES = 56 * 1024 * 1024
MXU_DTYPE = jnp.bfloat16


def _mm_kernel(a_ref, b_ref, o_ref):
    o_ref[...] = jnp.dot(a_ref[...].astype(MXU_DTYPE), b_ref[...].astype(MXU_DTYPE),
                         preferred_element_type=jnp.float32)


def _pick_tm(m):
    for tm in (1024, 512, 256, 128):
        if m % tm == 0:
            return tm
    return m


def mm(a, b, tn=1024):
    m, k = a.shape
    _, n = b.shape
    tm = _pick_tm(m)
    tn = min(tn, n)
    if m > tm:
        a = a.astype(MXU_DTYPE)
    if jnp.dtype(b.dtype).itemsize > 2:
        tn = min(tn, 512)
    return pl.pallas_call(
        _mm_kernel,
        out_shape=jax.ShapeDtypeStruct((m, n), jnp.float32),
        grid=(m // tm, pl.cdiv(n, tn)),
        in_specs=[pl.BlockSpec((tm, k), lambda i, j: (i, 0)),
                  pl.BlockSpec((k, tn), lambda i, j: (0, j))],
        out_specs=pl.BlockSpec((tm, tn), lambda i, j: (i, j)),
        compiler_params=pltpu.CompilerParams(
            dimension_semantics=("parallel", "arbitrary"),
            vmem_limit_bytes=VMEM_LIMIT_BYTES),
        name="mm",
    )(a, b)


def mm3(x, w):
    b, t, k = x.shape
    return mm(x.reshape(b * t, k), w).reshape(b, t, w.shape[1])


def rms_norm(x, g):
    xf = x.astype(jnp.float32)
    y = xf * lax.rsqrt(jnp.mean(xf * xf, axis=-1, keepdims=True) + NORM_EPS)
    return (y * g.astype(jnp.float32)).astype(x.dtype)


def masked_softmax(s, mask):
    s = s.astype(jnp.float32)
    m = jnp.max(jnp.where(mask, s, NEG_BIG), axis=-1, keepdims=True)
    e = jnp.where(mask, jnp.exp(s - m), 0.0)
    return e / jnp.maximum(jnp.sum(e, axis=-1, keepdims=True), 1e-30)


def rope_partial(x, pos):
    half = ROT_DIM // 2
    freqs = ROPE_THETA ** (-jnp.arange(half, dtype=jnp.float32) / half)
    ang = pos.astype(jnp.float32)[:, None] * freqs[None, :]
    cos = jnp.cos(ang)[None, :, None, :]
    sin = jnp.sin(ang)[None, :, None, :]
    xr = x[..., :ROT_DIM].astype(jnp.float32)
    x1, x2 = xr[..., :half], xr[..., half:]
    rot = jnp.concatenate([x1 * cos - x2 * sin, x1 * sin + x2 * cos], axis=-1)
    return jnp.concatenate([rot.astype(x.dtype), x[..., ROT_DIM:]], axis=-1)


def adaln(c_prompt, c_sample, w_ada, b_ada):
    bp, bs = c_prompt.shape[0], c_sample.shape[0]
    rows = bp + bs
    pad = (-rows) % 8
    c_all = jnp.concatenate([c_prompt, c_sample, jnp.zeros((pad, c_prompt.shape[1]), c_prompt.dtype)], axis=0)
    mod = mm(jax.nn.silu(c_all), w_ada) + b_ada
    return jnp.split(mod[:bp], 6, axis=-1), jnp.split(mod[bp:rows], 6, axis=-1)


def modulate(x, g, shift, scale):
    return rms_norm(x, g) * (1.0 + scale[:, None, :]) + shift[:, None, :]


def mixer_front(x, shift, scale, norm_g, w_in, q_norm, k_norm, gmlp_norm, pos):
    b, t, _ = x.shape
    h = modulate(x, norm_g, shift, scale)
    z = mm3(h, w_in)
    sizes = [BRANCH_WIDTH, 6 * KV_WIDTH, 3 * N_HEADS, GMLP_WIDTH, GMLP_WIDTH, 2 * CONV_WIDTH]
    zq, zkv, zg, zu, zv, zglu = jnp.split(z, np.cumsum(sizes)[:-1].tolist(), axis=-1)
    q = rms_norm(zq.reshape(b, t, N_HEADS, HEAD_DIM), q_norm)
    q_n = q.reshape(b, t, KV_HEADS, Q_PER_KV, HEAD_DIM)
    q_r = rope_partial(q, pos).reshape(b, t, KV_HEADS, Q_PER_KV, HEAD_DIM)
    kv = zkv.reshape(b, t, 3, 2, KV_HEADS, HEAD_DIM)
    cmp_kv = kv[:, :, 0]
    ks = rope_partial(rms_norm(kv[:, :, 1, 0], k_norm[1]), pos)
    slc_kv = jnp.stack([ks, kv[:, :, 1, 1]], axis=2)
    kw = rope_partial(rms_norm(kv[:, :, 2, 0], k_norm[2]), pos)
    win_kv = jnp.stack([kw, kv[:, :, 2, 1]], axis=2)
    gates = jax.nn.sigmoid(zg.reshape(b, t, KV_HEADS, Q_PER_KV, 3))
    u = jax.nn.gelu(zu)
    v = rms_norm(jax.nn.gelu(zv), gmlp_norm)
    return h, q_r, q_n, gates, zg, cmp_kv, slc_kv, win_kv, u, v, zglu


def compress(cmp_kv, pe, w1, b1, w2, kn):
    b, L = cmp_kv.shape[:2]
    n_seg = L // CMP_STRIDE
    nb = n_seg - CMP_RATIO + 1
    seg = cmp_kv[:, :n_seg * CMP_STRIDE].reshape(b, n_seg, CMP_STRIDE, 2, KV_HEADS, HEAD_DIM)
    w1r = w1.reshape(2, CMP_RATIO, CMP_STRIDE, HEAD_DIM, HEAD_DIM)
    acc = (jnp.einsum('cld,cldh->ch', pe, w1) + b1)[:, None, :]
    for r in range(CMP_RATIO):
        acc = acc + jnp.einsum('bnlcgd,cldh->bncgh', seg[:, r:r + nb], w1r[:, r])
    out = jnp.einsum('bncgh,chd->bncgd', jax.nn.gelu(acc), w2)
    kc = rms_norm(out[:, :, 0], kn)
    vc = out[:, :, 1]
    c_end = jnp.asarray(np.arange(nb) * CMP_STRIDE + CMP_BLOCK - 1, dtype=jnp.int32)
    return kc, vc, c_end


def slc_blocks(slc_kv):
    b, L = slc_kv.shape[:2]
    ns = -(-L // SLC_BLOCK)
    kv = jnp.pad(slc_kv, ((0, 0), (0, ns * SLC_BLOCK - L), (0, 0), (0, 0), (0, 0)))
    kv = kv.reshape(b, ns, SLC_BLOCK, 2, KV_HEADS, HEAD_DIM).transpose(3, 0, 4, 1, 2, 5)
    return kv[0], kv[1]


def overlap_matrix_np(nb, ns):
    cs = np.arange(nb) * CMP_STRIDE
    ce = cs + CMP_BLOCK - 1
    ss = np.arange(ns) * SLC_BLOCK
    se = ss + SLC_BLOCK - 1
    return ((cs[:, None] <= se[None, :]) & (ce[:, None] >= ss[None, :])).astype(np.float32)


def overlap_matrix(nb, ns):
    return jnp.asarray(overlap_matrix_np(nb, ns))


def nsa_attend(q_r, q_n, gates, pos_q, kc, vc, c_end, ovl, sk, sv, wk, wv, w_pos):
    b, tq = q_r.shape[:2]
    scale = HEAD_DIM ** -0.5
    s = jnp.einsum('bqgrd,bngd->bqgrn', q_n, kc) * scale
    p = masked_softmax(s, (c_end[None, :] <= pos_q[:, None])[None, :, None, None, :])
    o_c = jnp.einsum('bqgrn,bngd->bqgrd', p.astype(vc.dtype), vc)
    imp = jnp.einsum('bqgrn,ns->bgqs', p, ovl)
    ns = ovl.shape[1]
    j = jnp.arange(ns, dtype=jnp.int32)[None, :]
    cur = (pos_q // SLC_BLOCK)[:, None]
    valid = j <= cur
    forced = (j == 0) | (j == cur) | (j == cur - 1)
    imp = jnp.where(valid[None, None], jnp.where(forced[None, None], FORCE_SCORE, imp), -FORCE_SCORE)
    n_sel = min(N_SELECT, ns)
    sel = lax.top_k(imp, n_sel)[1]
    gather = jax.vmap(jax.vmap(lambda kb, si: kb[si]))
    ksel = gather(sk, sel)
    vsel = gather(sv, sel)
    tok = sel[..., None] * SLC_BLOCK + jnp.arange(SLC_BLOCK, dtype=jnp.int32)
    ms = (tok <= pos_q[None, None, :, None, None]).transpose(0, 2, 1, 3, 4).reshape(b, tq, KV_HEADS, 1, n_sel * SLC_BLOCK)
    s2 = jnp.einsum('bqgrd,bgqnld->bqgrnl', q_r, ksel).reshape(b, tq, KV_HEADS, Q_PER_KV, n_sel * SLC_BLOCK) * scale
    p2 = masked_softmax(s2, ms)
    o_s = jnp.einsum('bqgrk,bgqkd->bqgrd', p2.astype(vsel.dtype), vsel.reshape(b, KV_HEADS, tq, n_sel * SLC_BLOCK, HEAD_DIM))
    s3 = jnp.einsum('bqgrd,bkgd->bqgrk', q_r, wk) * scale
    dlt = pos_q[:, None] - w_pos[None, :]
    mw = (dlt >= 0) & (dlt < WINDOW) & (w_pos[None, :] >= 0)
    p3 = masked_softmax(s3, mw[None, :, None, None, :])
    o_w = jnp.einsum('bqgrk,bkgd->bqgrd', p3.astype(wv.dtype), wv)
    return gates[..., 0:1] * o_c + gates[..., 1:2] * o_s + gates[..., 2:3] * o_w


NSA_KT = 1024
_NT_DIMS = (((1,), (1,)), ((), ()))


def _softmax_tile_update(s, mask, v, sl, m_scr, l_scr, acc_scr):
    m_old = m_scr[sl]
    m_new = jnp.maximum(m_old, jnp.max(jnp.where(mask, s, NEG_BIG), axis=-1, keepdims=True))
    alpha = jnp.exp(m_old - m_new)
    e = jnp.where(mask, jnp.exp(s - m_new), 0.0)
    l_scr[sl] = alpha * l_scr[sl] + jnp.sum(e, axis=-1, keepdims=True)
    acc_scr[sl] = alpha * acc_scr[sl] + jnp.dot(e.astype(MXU_DTYPE), v, preferred_element_type=jnp.float32)
    m_scr[sl] = m_new


def _nsa_prompt_kernel(qn_ref, qr_ref, gate_ref, kc_ref, vc_ref, ovl_ref, eexp_ref,
                       ks_ref, vs_ref, kw_ref, vw_ref, o_ref, m_scr, l_scr, acc_scr):
    qi = pl.program_id(2)
    q0 = qi * Q_BLOCK
    rows = Q_PER_KV * Q_BLOCK
    scale = HEAD_DIM ** -0.5
    qn = (qn_ref[0, 0].reshape(rows, HEAD_DIM) * scale).astype(MXU_DTYPE)
    qr = (qr_ref[0, 0].reshape(rows, HEAD_DIM) * scale).astype(MXU_DTYPE)
    pos_row = q0 + (lax.broadcasted_iota(jnp.int32, (rows, 1), 0) & (Q_BLOCK - 1))

    kc = kc_ref[0, 0]
    ncp = kc.shape[0]
    s = lax.dot_general(qn, kc, _NT_DIMS, preferred_element_type=jnp.float32)
    c_end = lax.broadcasted_iota(jnp.int32, (rows, ncp), 1) * CMP_STRIDE + (CMP_BLOCK - 1)
    cmask = c_end <= pos_row
    m = jnp.max(jnp.where(cmask, s, NEG_BIG), axis=-1, keepdims=True)
    e = jnp.where(cmask, jnp.exp(s - m), 0.0)
    p = e / jnp.maximum(jnp.sum(e, axis=-1, keepdims=True), 1e-30)
    o_c = jnp.dot(p.astype(MXU_DTYPE), vc_ref[0, 0], preferred_element_type=jnp.float32)

    psum = p[0:Q_BLOCK]
    for r in range(1, Q_PER_KV):
        psum = psum + p[r * Q_BLOCK:(r + 1) * Q_BLOCK]
    p_hi = psum.astype(MXU_DTYPE)
    p_lo = (psum - p_hi.astype(jnp.float32)).astype(MXU_DTYPE)
    ovl = ovl_ref[...]
    imp = (jnp.dot(p_hi, ovl, preferred_element_type=jnp.float32)
           + jnp.dot(p_lo, ovl, preferred_element_type=jnp.float32))
    ns = imp.shape[1]
    j = lax.broadcasted_iota(jnp.int32, (Q_BLOCK, ns), 1)
    cur = (q0 + lax.broadcasted_iota(jnp.int32, (Q_BLOCK, 1), 0)) // SLC_BLOCK
    forced = (j == 0) | (j == cur) | (j == cur - 1)
    adj = jnp.where(j <= cur, jnp.where(forced, FORCE_SCORE, imp), -FORCE_SCORE)
    rank = jnp.zeros((Q_BLOCK, ns), jnp.int32)
    for k in range(ns):
        col = adj[:, k:k + 1]
        beats = (col > adj) | ((col == adj) & (j > k))
        rank = rank + beats.astype(jnp.int32)
    sel = (rank < N_SELECT).astype(MXU_DTYPE)

    def init():
        m_scr[...] = jnp.full(m_scr.shape, NEG_BIG, jnp.float32)
        l_scr[...] = jnp.zeros(l_scr.shape, jnp.float32)
        acc_scr[...] = jnp.zeros(acc_scr.shape, jnp.float32)

    def finish():
        return acc_scr[...] / jnp.maximum(l_scr[...], 1e-30)

    init()
    pos_q = pos_row[0:Q_BLOCK]
    all_rows = slice(0, rows)

    def slc_body(kt, carry):
        k0 = pl.multiple_of(kt * NSA_KT, NSA_KT)
        s2 = lax.dot_general(qr, ks_ref[0, 0, pl.ds(k0, NSA_KT), :], _NT_DIMS,
                             preferred_element_type=jnp.float32)
        bm = jnp.dot(sel, eexp_ref[kt], preferred_element_type=jnp.float32)
        kpos = k0 + lax.broadcasted_iota(jnp.int32, (Q_BLOCK, NSA_KT), 1)
        mask = (bm > 0.5) & (kpos <= pos_q)
        mask = jnp.concatenate([mask] * Q_PER_KV, axis=0)
        _softmax_tile_update(s2, mask, vs_ref[0, 0, pl.ds(k0, NSA_KT), :], all_rows, m_scr, l_scr, acc_scr)
        return carry

    lax.fori_loop(0, (q0 + Q_BLOCK + NSA_KT - 1) // NSA_KT, slc_body, 0)
    o_s = finish()

    init()
    w0 = pl.multiple_of(jnp.maximum(q0 - WINDOW, 0), Q_BLOCK)
    span = WINDOW + Q_BLOCK
    s3 = lax.dot_general(qr, kw_ref[0, 0, pl.ds(w0, span), :], _NT_DIMS,
                         preferred_element_type=jnp.float32)
    dlt = pos_q - (w0 + lax.broadcasted_iota(jnp.int32, (Q_BLOCK, span), 1))
    wmask = (dlt >= 0) & (dlt < WINDOW)
    wmask = jnp.concatenate([wmask] * Q_PER_KV, axis=0)
    _softmax_tile_update(s3, wmask, vw_ref[0, 0, pl.ds(w0, span), :], all_rows, m_scr, l_scr, acc_scr)
    o_w = finish()

    gate = jax.nn.sigmoid(gate_ref[0, 0])
    outs = []
    for r in range(Q_PER_KV):
        sl = slice(r * Q_BLOCK, (r + 1) * Q_BLOCK)
        outs.append(gate[:, 3 * r:3 * r + 1] * o_c[sl] + gate[:, 3 * r + 1:3 * r + 2] * o_s[sl]
                    + gate[:, 3 * r + 2:3 * r + 3] * o_w[sl])
    o_ref[0] = jnp.concatenate(outs, axis=1).astype(o_ref.dtype)


def nsa_prompt(q_r, q_n, zg, kc, vc, slc_kv, win_kv):
    b, t = q_r.shape[:2]
    assert t % NSA_KT == 0 and t >= WINDOW + Q_BLOCK and WINDOW % Q_BLOCK == 0
    nb = kc.shape[1]
    ncp = -(-nb // 128) * 128
    ns = t // SLC_BLOCK
    heads_first = lambda a: a.transpose(0, 2, 3, 1, 4)
    kv_first = lambda a: a.transpose(0, 2, 1, 3).astype(MXU_DTYPE)
    pad_c = lambda a: jnp.pad(a, ((0, 0), (0, ncp - nb), (0, 0), (0, 0)))
    gate_in = zg.reshape(b, t, KV_HEADS, 3 * Q_PER_KV).transpose(0, 2, 1, 3)
    ovl = np.zeros((ncp, ns), np.float32)
    ovl[:nb] = np.asarray(overlap_matrix_np(nb, ns))
    key_blk = (np.arange(t) // SLC_BLOCK).reshape(t // NSA_KT, 1, NSA_KT)
    eexp = (key_blk == np.arange(ns).reshape(1, ns, 1)).astype(np.float32)
    rows = Q_PER_KV * Q_BLOCK
    q_spec = pl.BlockSpec((1, 1, Q_PER_KV, Q_BLOCK, HEAD_DIM), lambda bi, g, qi: (bi, g, 0, qi, 0))
    c_spec = pl.BlockSpec((1, 1, ncp, HEAD_DIM), lambda bi, g, qi: (bi, g, 0, 0))
    kv_spec = pl.BlockSpec((1, 1, t, HEAD_DIM), lambda bi, g, qi: (bi, g, 0, 0))
    return pl.pallas_call(
        _nsa_prompt_kernel,
        out_shape=jax.ShapeDtypeStruct((b, t, BRANCH_WIDTH), MXU_DTYPE),
        grid=(b, KV_HEADS, t // Q_BLOCK),
        in_specs=[q_spec, q_spec,
                  pl.BlockSpec((1, 1, Q_BLOCK, 3 * Q_PER_KV), lambda bi, g, qi: (bi, g, qi, 0)),
                  c_spec, c_spec,
                  pl.BlockSpec((ncp, ns), lambda bi, g, qi: (0, 0)),
                  pl.BlockSpec((t // NSA_KT, ns, NSA_KT), lambda bi, g, qi: (0, 0, 0)),
                  kv_spec, kv_spec, kv_spec, kv_spec],
        out_specs=pl.BlockSpec((1, Q_BLOCK, Q_PER_KV * HEAD_DIM), lambda bi, g, qi: (bi, qi, g)),
        scratch_shapes=[pltpu.VMEM((rows, 1), jnp.float32),
                        pltpu.VMEM((rows, 1), jnp.float32),
                        pltpu.VMEM((rows, HEAD_DIM), jnp.float32)],
        compiler_params=pltpu.CompilerParams(
            dimension_semantics=("parallel", "parallel", "arbitrary"),
            vmem_limit_bytes=VMEM_LIMIT_BYTES),
        name="nsa_prompt",
    )(heads_first(q_n), heads_first(q_r), gate_in,
      kv_first(pad_c(kc)), kv_first(pad_c(vc)),
      jnp.asarray(ovl, MXU_DTYPE), jnp.asarray(eexp, MXU_DTYPE),
      kv_first(slc_kv[:, :, 0]), kv_first(slc_kv[:, :, 1]),
      kv_first(win_kv[:, :, 0]), kv_first(win_kv[:, :, 1]))


SEGS_PER_PAGE = PAGE_SIZE // CMP_STRIDE
SEG_WIDTH = CMP_STRIDE * HEAD_DIM
GD = KV_HEADS * HEAD_DIM


def _masked_softmax_parts(s, mask, s_new, new_on):
    m = jnp.max(jnp.where(mask, s, NEG_BIG), axis=-1, keepdims=True)
    m = jnp.maximum(m, jnp.where(new_on, s_new, NEG_BIG))
    e = jnp.where(mask, jnp.exp(s - m), 0.0)
    e_new = jnp.where(new_on, jnp.exp(s_new - m), 0.0)
    den = jnp.maximum(jnp.sum(e, axis=-1, keepdims=True) + e_new, 1e-30)
    return e, e_new, den


def _nsa_sample_kernel(pt_ref, qn_ref, qr_ref, gate_ref, new_ref, w1_ref, b0_ref, w2_ref, kn_ref,
                       ovl_ref, eexp_ref, win_ref, *rest, n_pages, past_len, win_len):
    cmp_refs = rest[:n_pages]
    slc_refs = rest[n_pages:2 * n_pages]
    o_ref = rest[2 * n_pages]
    nh = N_HEADS
    n_seg = n_pages * SEGS_PER_PAGE
    row_group = lax.broadcasted_iota(jnp.int32, (nh, 1), 0) // Q_PER_KV

    outs = []
    for c in range(2):
        x = jnp.concatenate(
            [jnp.concatenate([cmp_refs[p][0, 0, c, g] for p in range(n_pages)], axis=0)
             for g in range(KV_HEADS)], axis=0).astype(MXU_DTYPE)
        ab = jnp.dot(x, w1_ref[c], preferred_element_type=jnp.float32)
        a = ab[:, :HEAD_DIM]
        b_next = pltpu.roll(ab[:, HEAD_DIM:], KV_HEADS * n_seg - 1, axis=0)
        hid = jax.nn.gelu(a + b_next + b0_ref[c])
        outs.append(jnp.dot(hid.astype(MXU_DTYPE), w2_ref[c].astype(MXU_DTYPE),
                            preferred_element_type=jnp.float32))
    kc_all = outs[0]
    kc_all = kc_all * lax.rsqrt(jnp.mean(kc_all * kc_all, axis=-1, keepdims=True) + NORM_EPS) * kn_ref[...]
    vc_all = outs[1]

    qn = qn_ref[0].astype(MXU_DTYPE)
    s_c = jnp.zeros((nh, n_seg), jnp.float32)
    for g in range(KV_HEADS):
        kc_g = kc_all[g * n_seg:(g + 1) * n_seg].astype(MXU_DTYPE)
        s_g = lax.dot_general(qn, kc_g, _NT_DIMS, preferred_element_type=jnp.float32)
        s_c = jnp.where(row_group == g, s_g, s_c)
    blk = lax.broadcasted_iota(jnp.int32, (nh, n_seg), 1)
    cmask = (blk * CMP_STRIDE + (CMP_BLOCK - 1) <= past_len) & (blk < n_seg - CMP_RATIO + 1)
    m = jnp.max(jnp.where(cmask, s_c, NEG_BIG), axis=-1, keepdims=True)
    e = jnp.where(cmask, jnp.exp(s_c - m), 0.0)
    p = e / jnp.maximum(jnp.sum(e, axis=-1, keepdims=True), 1e-30)
    o_c = jnp.zeros((nh, HEAD_DIM), jnp.float32)
    p_bf = p.astype(MXU_DTYPE)
    psum = jnp.zeros((nh, n_seg), jnp.float32)
    for g in range(KV_HEADS):
        vc_g = vc_all[g * n_seg:(g + 1) * n_seg].astype(MXU_DTYPE)
        in_g = row_group == g
        o_c = jnp.where(in_g, jnp.dot(p_bf, vc_g, preferred_element_type=jnp.float32), o_c)
        psum = jnp.where(in_g, jnp.sum(jnp.where(in_g, p, 0.0), axis=0, keepdims=True), psum)

    p_hi = psum.astype(MXU_DTYPE)
    p_lo = (psum - p_hi.astype(jnp.float32)).astype(MXU_DTYPE)
    ovl = ovl_ref[...]
    imp = (jnp.dot(p_hi, ovl, preferred_element_type=jnp.float32)
           + jnp.dot(p_lo, ovl, preferred_element_type=jnp.float32))
    ns_pad = imp.shape[1]
    ns = past_len // SLC_BLOCK + 1
    cur = past_len // SLC_BLOCK
    j = lax.broadcasted_iota(jnp.int32, (nh, ns_pad), 1)
    forced = (j == 0) | (j == cur) | (j == cur - 1)
    adj = jnp.where(j <= cur, jnp.where(forced, FORCE_SCORE, imp), -FORCE_SCORE)
    adj = jnp.where(j < ns, adj, PEER_NEG)
    rank = jnp.zeros((nh, ns_pad), jnp.int32)
    for k in range(ns):
        col = adj[:, k:k + 1]
        beats = (col > adj) | ((col == adj) & (j > k))
        rank = rank + beats.astype(jnp.int32)
    sel = (rank < N_SELECT) & (j < ns)
    tok_mask = jnp.dot(sel.astype(MXU_DTYPE), eexp_ref[...], preferred_element_type=jnp.float32) > 0.5
    new_sel = jnp.sum(jnp.where(sel & (j == cur), 1.0, 0.0), axis=-1, keepdims=True) > 0.5

    qr = qr_ref[0]
    qr_bf = qr.astype(MXU_DTYPE)
    new = new_ref[0]
    to_mxu = lambda a: a.astype(MXU_DTYPE).astype(jnp.float32)
    s_new = jnp.sum(to_mxu(qr) * to_mxu(new[0:1]), axis=-1, keepdims=True)
    s_sel = jnp.concatenate(
        [jnp.dot(qr_bf, slc_refs[pg][0, 0, 0:GD, :].astype(MXU_DTYPE), preferred_element_type=jnp.float32)
         for pg in range(n_pages)], axis=1)
    e, e_new, den = _masked_softmax_parts(s_sel, tok_mask, s_new, new_sel)
    e_bf = e.astype(MXU_DTYPE)
    o_s = to_mxu(e_new) * to_mxu(new[1:2])
    for pg in range(n_pages):
        o_s = o_s + lax.dot_general(e_bf[:, pg * PAGE_SIZE:(pg + 1) * PAGE_SIZE],
                                    slc_refs[pg][0, 0, GD:2 * GD, :].astype(MXU_DTYPE), _NT_DIMS,
                                    preferred_element_type=jnp.float32)
    o_s = o_s / den

    sw_new = jnp.sum(to_mxu(qr) * to_mxu(new[2:3]), axis=-1, keepdims=True)
    s_w = jnp.dot(qr_bf, win_ref[0, 0, 0:GD, :].astype(MXU_DTYPE), preferred_element_type=jnp.float32)
    dlt = win_len - lax.broadcasted_iota(jnp.int32, (nh, win_len), 1)
    wmask = (dlt >= 0) & (dlt < WINDOW) & (past_len - dlt >= 0)
    e, e_new, den = _masked_softmax_parts(s_w, wmask, sw_new, jnp.full((nh, 1), True))
    o_w = (to_mxu(e_new) * to_mxu(new[3:4])
           + lax.dot_general(e.astype(MXU_DTYPE), win_ref[0, 0, GD:2 * GD, :].astype(MXU_DTYPE), _NT_DIMS,
                             preferred_element_type=jnp.float32)) / den

    def own_block(o_bd):
        out = jnp.zeros((nh, HEAD_DIM), jnp.float32)
        for g in range(KV_HEADS):
            out = jnp.where(row_group == g, o_bd[:, g * HEAD_DIM:(g + 1) * HEAD_DIM], out)
        return out

    gate = jax.nn.sigmoid(gate_ref[0])
    o_ref[0] = gate[:, 0:1] * o_c + gate[:, 1:2] * own_block(o_s) + gate[:, 2:3] * own_block(o_w)


def nsa_sample(q_r, q_n, zg, skv_new, wkv_new, cmp_pages, slc_pages, win_t, page_table, layer,
               cmp_pe, cmp_w1, cmp_b1, cmp_w2, kn):
    bs = q_r.shape[0]
    n_pages = page_table.shape[1]
    past_len = n_pages * PAGE_SIZE
    win_len = win_t.shape[-1]
    n_seg = n_pages * SEGS_PER_PAGE
    nb = n_seg - CMP_RATIO + 1
    ns = past_len // SLC_BLOCK + 1
    ns_pad = -(-ns // 128) * 128
    scale = HEAD_DIM ** -0.5
    qn = q_n.reshape(bs, N_HEADS, HEAD_DIM) * scale
    eye = jnp.eye(KV_HEADS, dtype=q_r.dtype)
    qr_bd = (q_r.reshape(bs, KV_HEADS, Q_PER_KV, 1, HEAD_DIM) * scale
             * eye[None, :, None, :, None]).reshape(bs, N_HEADS, GD)
    gate_in = zg.reshape(bs, N_HEADS, 3)
    new = jnp.concatenate([skv_new.reshape(bs, 2, GD), wkv_new.reshape(bs, 2, GD)], axis=1)
    w1r = cmp_w1.reshape(2, CMP_RATIO, SEG_WIDTH, HEAD_DIM)
    w1c = jnp.concatenate([w1r[:, r] for r in range(CMP_RATIO)], axis=-1).astype(MXU_DTYPE)
    b0 = (jnp.einsum('cld,cldh->ch', cmp_pe, cmp_w1, precision=lax.Precision.HIGHEST) + cmp_b1)[:, None, :]
    ovl = np.zeros((n_seg, ns_pad), np.float32)
    ovl[:nb, :ns] = overlap_matrix_np(nb, ns)
    eexp = (np.arange(past_len)[None, :] // SLC_BLOCK == np.arange(ns_pad)[:, None]).astype(np.float32)
    const = lambda shape: pl.BlockSpec(shape, lambda b, pt: (0,) * len(shape))
    per_b = lambda shape: pl.BlockSpec((1,) + shape, lambda b, pt: (b,) + (0,) * len(shape))
    cmp_spec = [pl.BlockSpec((1, 1, 2, KV_HEADS, SEGS_PER_PAGE, SEG_WIDTH),
                             functools.partial(lambda b, pt, pg: (layer, pt[b, pg], 0, 0, 0, 0), pg=pg))
                for pg in range(n_pages)]
    slc_spec = [pl.BlockSpec((1, 1, 2 * GD, PAGE_SIZE),
                             functools.partial(lambda b, pt, pg: (layer, pt[b, pg], 0, 0), pg=pg))
                for pg in range(n_pages)]
    grid_spec = pltpu.PrefetchScalarGridSpec(
        num_scalar_prefetch=1,
        grid=(bs,),
        in_specs=[per_b((N_HEADS, HEAD_DIM)), per_b((N_HEADS, GD)), per_b((N_HEADS, 3)), per_b((4, GD)),
                  const((2, SEG_WIDTH, CMP_RATIO * HEAD_DIM)), const((2, 1, HEAD_DIM)),
                  const((2, HEAD_DIM, HEAD_DIM)), const((1, HEAD_DIM)),
                  const((n_seg, ns_pad)), const((ns_pad, past_len)),
                  pl.BlockSpec((1, 1, 2 * GD, win_len), lambda b, pt: (layer, b, 0, 0))]
                 + cmp_spec + slc_spec,
        out_specs=per_b((N_HEADS, HEAD_DIM)),
    )
    out = pl.pallas_call(
        functools.partial(_nsa_sample_kernel, n_pages=n_pages, past_len=past_len, win_len=win_len),
        out_shape=jax.ShapeDtypeStruct((bs, N_HEADS, HEAD_DIM), jnp.float32),
        grid_spec=grid_spec,
        compiler_params=pltpu.CompilerParams(
            dimension_semantics=("arbitrary",),
            vmem_limit_bytes=VMEM_LIMIT_BYTES),
        name="nsa_sample",
    )(page_table, qn, qr_bd, gate_in, new, w1c, b0, cmp_w2, kn.reshape(1, HEAD_DIM),
      jnp.asarray(ovl, MXU_DTYPE), jnp.asarray(eexp, MXU_DTYPE), win_t,
      *([cmp_pages] * n_pages), *([slc_pages] * n_pages))
    return out.reshape(bs, 1, BRANCH_WIDTH)


def gmlp_mix(u, v, w_s, b_s):
    b, t, _ = v.shape
    nc = -(-t // CHUNK)
    vp = jnp.pad(v, ((0, 0), (0, nc * CHUNK - t), (0, 0))).reshape(b, nc, CHUNK, GMLP_GROUPS, GMLP_WIDTH // GMLP_GROUPS)
    w = w_s * jnp.tril(jnp.ones((CHUNK, CHUNK), w_s.dtype))
    mixed = jnp.einsum('gpq,bcqgd->bcpgd', w, vp) + b_s.T[None, None, :, :, None]
    return u * mixed.reshape(b, nc * CHUNK, GMLP_WIDTH)[:, :t]


def conformer_conv(zglu, buf, w_dw, b_dw, norm_g):
    a, gt = jnp.split(zglu, 2, axis=-1)
    glu = a * jax.nn.sigmoid(gt)
    xin = jnp.concatenate([buf.astype(glu.dtype), glu], axis=1)
    y = lax.conv_general_dilated(xin, w_dw[:, None, :].astype(glu.dtype), (1,), 'VALID',
                                 dimension_numbers=('NWC', 'WIO', 'NWC'), feature_group_count=CONV_WIDTH) + b_dw
    y = jax.nn.silu(rms_norm(y, norm_g))
    return y, xin[:, xin.shape[1] - (CONV_K - 1):]


MERGE_TN = 512


def _merge_kernel(h_ref, o0_ref, o1_ref, o2_ref, wm0_ref, wm1_ref, wm2_ref, bm0_ref, bm1_ref, bm2_ref,
                  wb_ref, y_ref):
    h = h_ref[...].astype(MXU_DTYPE)
    acc = None
    for i, (o_ref, wm_ref, bm_ref) in enumerate(((o0_ref, wm0_ref, bm0_ref), (o1_ref, wm1_ref, bm1_ref),
                                                 (o2_ref, wm2_ref, bm2_ref))):
        mg = jax.nn.sigmoid(jnp.dot(h, wm_ref[...].astype(MXU_DTYPE), preferred_element_type=jnp.float32)
                            + bm_ref[...])
        term = mg * jnp.dot(o_ref[...].astype(MXU_DTYPE), wb_ref[i].astype(MXU_DTYPE),
                            preferred_element_type=jnp.float32)
        acc = term if acc is None else acc + term
    y_ref[...] = acc.astype(y_ref.dtype)


def merge_branches(h, o_nsa, o_gmlp, o_conv, w_branch, w_merge, b_merge):
    m, d = h.shape
    tm = 512 if m % 512 == 0 else m
    tn = MERGE_TN
    nj = d // tn
    bw = o_nsa.shape[1]
    b2 = b_merge.reshape(1, 3 * d)
    o_spec = pl.BlockSpec((tm, bw), lambda i, j: (i, 0))
    wm_specs = [pl.BlockSpec((d, tn), functools.partial(lambda i, j, k: (0, k * nj + j), k=k)) for k in range(3)]
    bm_specs = [pl.BlockSpec((1, tn), functools.partial(lambda i, j, k: (0, k * nj + j), k=k)) for k in range(3)]
    return pl.pallas_call(
        _merge_kernel,
        out_shape=jax.ShapeDtypeStruct((m, d), MXU_DTYPE),
        grid=(m // tm, nj),
        in_specs=[pl.BlockSpec((tm, d), lambda i, j: (i, 0)), o_spec, o_spec, o_spec]
                 + wm_specs + bm_specs + [pl.BlockSpec((3, bw, tn), lambda i, j: (0, 0, j))],
        out_specs=pl.BlockSpec((tm, tn), lambda i, j: (i, j)),
        compiler_params=pltpu.CompilerParams(
            dimension_semantics=("parallel", "arbitrary"),
            vmem_limit_bytes=VMEM_LIMIT_BYTES),
        name="merge_branches",
    )(h, o_nsa, o_gmlp, o_conv, w_merge, w_merge, w_merge, b2, b2, b2, w_branch)


def mixer_merge(x, h, o_nsa, o_gmlp, o_conv, gate, w_branch, w_merge, b_merge, w_out):
    b, t, d = x.shape
    flat = lambda a: a.reshape(b * t, a.shape[-1]).astype(MXU_DTYPE)
    y = merge_branches(flat(h), flat(o_nsa), flat(o_gmlp), flat(o_conv), w_branch, w_merge, b_merge)
    return x + gate[:, None, :] * mm(y, w_out).reshape(b, t, d)


PEER_TB = 512
PEER_I1 = 8
PEER_JSUB = 4
PEER_KSUB = 32
PEER_NEG = -3.0e38
PEER_NRANK = PEER_TOPK + 1
PEER_VROWS = 24
PEER_NCAND = PEER_VROWS + 7 * 8 + 16


def _extract_top(cur_ref, out_ref, n_out):
    n = cur_ref.shape[0]
    iota = lax.broadcasted_iota(jnp.int32, cur_ref.shape, 0)
    for k in range(n_out):
        cur = cur_ref[...]
        m = jnp.max(cur, axis=0, keepdims=True)
        first = jnp.min(jnp.where(cur == m, iota, n), axis=0, keepdims=True)
        cur_ref[...] = jnp.where(iota == first, PEER_NEG, cur)
        out_ref[k:k + 1, :] = m


def _peer_route_kernel(x_ref, w_ref, k_ref, thr1_ref, e1_ref, s2_ref, e2_ref,
                       work_ref, v1_ref, v2_ref, cand_ref, ctop_ref):
    x = x_ref[...].astype(MXU_DTYPE)
    w = w_ref[...].astype(MXU_DTYPE)
    q = jnp.dot(x, w, preferred_element_type=jnp.float32)
    half = PEER_QDIM // 2
    nt = (((1,), (1,)), ((), ()))
    s = []
    for c in range(2):
        qc = q[:, c * half:(c + 1) * half].astype(MXU_DTYPE)
        kc = k_ref[0, c].astype(MXU_DTYPE)
        s.append(lax.dot_general(kc, qc, nt, preferred_element_type=jnp.float32))
    s1, s2 = s
    v1_ref[...] = jnp.full(v1_ref.shape, PEER_NEG, jnp.float32)
    v2_ref[...] = jnp.full(v2_ref.shape, PEER_NEG, jnp.float32)
    work_ref[...] = s1
    _extract_top(work_ref, v1_ref, PEER_NRANK)
    work_ref[...] = s2
    _extract_top(work_ref, v2_ref, PEER_NRANK)
    v1 = v1_ref[...]
    v2 = v2_ref[...]
    nv = PEER_VROWS
    cand_ref[0:nv, :] = v1[0:1] + v2
    for a in range(1, 8):
        cand_ref[nv + 8 * (a - 1):nv + 8 * a, :] = v1[a:a + 1] + v2[0:8]
    cand_ref[nv + 56:nv + 72, :] = v1[8:nv] + v2[0:1]
    _extract_top(cand_ref, ctop_ref, PEER_NRANK)
    tau = 0.5 * (ctop_ref[PEER_TOPK - 1:PEER_TOPK, :] + ctop_ref[PEER_TOPK:PEER_TOPK + 1, :])
    m1 = v1[0:1]
    m2 = v2[0:1]
    e1top = jnp.exp(v1 - m1)
    e2top = jnp.exp(v2 - m2)
    z = jnp.zeros_like(tau)
    for a in range(PEER_NRANK):
        sel = v2 >= (tau - v1[a:a + 1])
        z = z + jnp.sum(jnp.where(sel, e2top, 0.0), axis=0, keepdims=True) * e1top[a:a + 1]
    inv_z = 1.0 / z
    thr1_ref[0] = tau - s1
    e1_ref[0] = jnp.exp(s1 - m1) * inv_z
    s2_ref[0] = s2
    e2_ref[0] = jnp.exp(s2 - m2)


def peer_route(xm, w_pq, sub_keys):
    t, d = xm.shape
    tb = PEER_TB
    out = jax.ShapeDtypeStruct((PEER_HEADS, PEER_KEYS, t), jnp.float32)
    ospec = pl.BlockSpec((1, PEER_KEYS, tb), lambda i, h: (h, 0, i))
    return pl.pallas_call(
        _peer_route_kernel,
        out_shape=(out, out, out, out),
        grid=(t // tb, PEER_HEADS),
        in_specs=[pl.BlockSpec((tb, d), lambda i, h: (i, 0)),
                  pl.BlockSpec((d, PEER_QDIM), lambda i, h: (0, h)),
                  pl.BlockSpec((1, 2, PEER_KEYS, PEER_QDIM // 2), lambda i, h: (h, 0, 0, 0))],
        out_specs=(ospec, ospec, ospec, ospec),
        scratch_shapes=[pltpu.VMEM((PEER_KEYS, tb), jnp.float32),
                        pltpu.VMEM((PEER_VROWS, tb), jnp.float32),
                        pltpu.VMEM((PEER_VROWS, tb), jnp.float32),
                        pltpu.VMEM((PEER_NCAND, tb), jnp.float32),
                        pltpu.VMEM((PEER_VROWS, tb), jnp.float32)],
        compiler_params=pltpu.CompilerParams(
            dimension_semantics=("parallel", "arbitrary"),
            vmem_limit_bytes=VMEM_LIMIT_BYTES),
        name="peer_route",
    )(xm, w_pq, sub_keys)


def _peer_dense_kernel(xt_ref, u_ref, vt_ref, thr1_ref, e1_ref, s2_ref, e2_ref, o_ref, s_cur, w_new):
    @pl.when(pl.program_id(1) == 0)
    def _():
        o_ref[...] = jnp.zeros_like(o_ref)

    tb = o_ref.shape[1]
    nk = PEER_KEYS
    ks = PEER_KSUB
    s_cur[...] = jnp.dot(u_ref[...], xt_ref[...], preferred_element_type=jnp.float32)
    for t0 in range(0, tb, 128):
        lanes = slice(t0, t0 + 128)
        for k0 in range(0, nk, ks):
            for j0 in range(0, PEER_I1, PEER_JSUB):
                g = [jnp.zeros((ks, 128), jnp.float32) for _ in range(PEER_JSUB)]
                for h in range(PEER_HEADS):
                    s2 = s2_ref[h, k0:k0 + ks, lanes]
                    e2 = e2_ref[h, k0:k0 + ks, lanes]
                    for jj in range(PEER_JSUB):
                        j = j0 + jj
                        sel = s2 >= thr1_ref[h, j:j + 1, lanes]
                        g[jj] = g[jj] + jnp.where(sel, e2, 0.0) * e1_ref[h, j:j + 1, lanes]
                for jj in range(PEER_JSUB):
                    r0 = (j0 + jj) * nk + k0
                    act = jax.nn.gelu(s_cur[r0:r0 + ks, lanes])
                    w_new[r0:r0 + ks, lanes] = (act * g[jj]).astype(MXU_DTYPE)
    o_ref[...] += jnp.dot(vt_ref[...], w_new[...], preferred_element_type=jnp.float32)


def peer_dense(xt, u_bf, vt_bf, thr1, e1, s2, e2):
    d, t = xt.shape
    n_exp = u_bf.shape[0]
    tb = PEER_TB
    ec = PEER_I1 * PEER_KEYS
    row_spec = pl.BlockSpec((PEER_HEADS, PEER_I1, tb), lambda i, c: (0, c, i))
    full_spec = pl.BlockSpec((PEER_HEADS, PEER_KEYS, tb), lambda i, c: (0, 0, i))
    return pl.pallas_call(
        _peer_dense_kernel,
        out_shape=jax.ShapeDtypeStruct((d, t), jnp.float32),
        grid=(t // tb, n_exp // ec),
        in_specs=[pl.BlockSpec((d, tb), lambda i, c: (0, i)),
                  pl.BlockSpec((ec, d), lambda i, c: (c, 0)),
                  pl.BlockSpec((d, ec), lambda i, c: (0, c)),
                  row_spec, row_spec, full_spec, full_spec],
        out_specs=pl.BlockSpec((d, tb), lambda i, c: (0, i)),
        scratch_shapes=[pltpu.VMEM((ec, tb), jnp.float32), pltpu.VMEM((ec, tb), MXU_DTYPE)],
        compiler_params=pltpu.CompilerParams(
            dimension_semantics=("parallel", "arbitrary"),
            vmem_limit_bytes=VMEM_LIMIT_BYTES),
        name="peer_dense",
    )(xt, u_bf, vt_bf, thr1, e1, s2, e2)


def peer_ffn(xm, w_pq, sub_keys, u_bf, vt_bf):
    t, d = xm.shape
    tp = -(-t // PEER_TB) * PEER_TB
    xm_p = jnp.pad(xm, ((0, tp - t), (0, 0)))
    thr1, e1, s2, e2 = peer_route(xm_p, w_pq, sub_keys)
    out_t = peer_dense(xm_p.T.astype(MXU_DTYPE), u_bf, vt_bf, thr1, e1, s2, e2)
    return out_t.T[:t]


def kernel(x_prompt, x_sample, cache_cmp_kv, cache_slc_kv, state_win_kv, state_conv, page_table,
           c_prompt, c_sample, w_ada, b_ada, norm_mix, norm_ffn, w_in, q_norm, k_norm,
           cmp_pe, cmp_w1, cmp_b1, cmp_w2, gmlp_norm, gmlp_ws, gmlp_bs, conv_w, conv_b, conv_norm,
           w_branch, w_merge, b_merge, w_out, peer_wq, peer_keys, peer_u, peer_v):
    bp, t = x_prompt.shape[:2]
    bs, ds = x_sample.shape[:2]
    depth = w_in.shape[0]
    past_len = page_table.shape[1] * PAGE_SIZE
    pos_p = jnp.arange(t, dtype=jnp.int32)
    pos_s = past_len + jnp.arange(ds, dtype=jnp.int32)
    xp, xs = x_prompt, x_sample
    assert ds == 1
    n_phys = cache_cmp_kv.shape[1]
    cmp_pages = cache_cmp_kv.reshape(depth, n_phys, SEGS_PER_PAGE, CMP_STRIDE, 2, KV_HEADS, HEAD_DIM)
    cmp_pages = cmp_pages.transpose(0, 1, 4, 5, 2, 3, 6).reshape(
        depth, n_phys, 2, KV_HEADS, SEGS_PER_PAGE, SEG_WIDTH)
    slc_pages = cache_slc_kv.transpose(0, 1, 3, 4, 5, 2).reshape(depth, n_phys, 2 * GD, PAGE_SIZE)
    win_t = state_win_kv.transpose(0, 1, 3, 4, 5, 2).reshape(depth, bs, 2 * GD, state_win_kv.shape[2])
    cmp_p, cmp_s, slc_p, slc_s, win_p, win_s, conv_p, conv_s, gv_s = [], [], [], [], [], [], [], [], []
    for l in range(depth):
        mp, msm = adaln(c_prompt, c_sample, w_ada[l], b_ada[l])
        w_in_l, w_merge_l, w_branch_l, w_out_l = (w.astype(MXU_DTYPE) for w in
                                                  (w_in[l], w_merge[l], w_branch[l], w_out[l]))
        hp, q_r, q_n, gates, zg, ckv, skv, wkv, u, v, zglu = mixer_front(
            xp, mp[0], mp[1], norm_mix[l], w_in_l, q_norm[l], k_norm[l], gmlp_norm[l], pos_p)
        kc, vc, c_end = compress(ckv, cmp_pe[l], cmp_w1[l], cmp_b1[l], cmp_w2[l], k_norm[l, 0])
        o_nsa = nsa_prompt(q_r, q_n, zg, kc, vc, skv, wkv)
        o_g = gmlp_mix(u, v, gmlp_ws[l], gmlp_bs[l])
        o_c, buf = conformer_conv(zglu, jnp.zeros((bp, CONV_K - 1, CONV_WIDTH), zglu.dtype),
                                  conv_w[l], conv_b[l], conv_norm[l])
        xp = mixer_merge(xp, hp, o_nsa, o_g, o_c, mp[2], w_branch_l, w_merge_l, b_merge[l], w_out_l)
        cmp_p.append(ckv)
        slc_p.append(skv)
        win_p.append(wkv[:, t - min(WINDOW, t):])
        conv_p.append(buf)
        hs, q_r, q_n, gates, zg, ckv, skv, wkv, u, v, zglu = mixer_front(
            xs, msm[0], msm[1], norm_mix[l], w_in_l, q_norm[l], k_norm[l], gmlp_norm[l], pos_s)
        win_full = jnp.concatenate([state_win_kv[l], wkv], axis=1)
        o_nsa = nsa_sample(q_r, q_n, zg, skv, wkv, cmp_pages, slc_pages, win_t, page_table, l,
                           cmp_pe[l], cmp_w1[l], cmp_b1[l], cmp_w2[l], k_norm[l, 0])
        o_g = gmlp_mix(u, v, gmlp_ws[l], gmlp_bs[l])
        o_c, buf = conformer_conv(zglu, state_conv[l], conv_w[l], conv_b[l], conv_norm[l])
        xs = mixer_merge(xs, hs, o_nsa, o_g, o_c, msm[2], w_branch_l, w_merge_l, b_merge[l], w_out_l)
        d = xp.shape[-1]
        xm_all = jnp.concatenate([modulate(xp, norm_ffn[l], mp[3], mp[4]).reshape(bp * t, d),
                                  modulate(xs, norm_ffn[l], msm[3], msm[4]).reshape(bs * ds, d)], axis=0)
        ffn = peer_ffn(xm_all, peer_wq[l].astype(MXU_DTYPE), peer_keys[l],
                       peer_u[l].astype(MXU_DTYPE), peer_v[l].T.astype(MXU_DTYPE))
        xp = xp + mp[5][:, None, :] * ffn[:bp * t].reshape(bp, t, d)
        xs = xs + msm[5][:, None, :] * ffn[bp * t:].reshape(bs, ds, d)
        cmp_s.append(ckv)
        slc_s.append(skv)
        lw = win_full.shape[1]
        win_s.append(win_full[:, lw - min(WINDOW, lw):])
        conv_s.append(buf)
        gv_s.append(v)
    return (xp, xs, jnp.stack(cmp_p), jnp.stack(cmp_s), jnp.stack(slc_p), jnp.stack(slc_s),
            jnp.stack(win_p), jnp.stack(win_s), jnp.stack(conv_p), jnp.stack(conv_s), jnp.stack(gv_s))
```

```python
import functools

import jax
import jax.numpy as jnp
import numpy as np
from jax import lax
from jax.experimental import pallas as pl
from jax.experimental.pallas import tpu as pltpu

D_MODEL = 2048
PAGE_SIZE = 128
BRANCH_WIDTH = D_MODEL // 2
HEAD_DIM = 64
N_HEADS = BRANCH_WIDTH // HEAD_DIM
KV_HEADS = 4
Q_PER_KV = N_HEADS // KV_HEADS
KV_WIDTH = KV_HEADS * HEAD_DIM
ROT_DIM = HEAD_DIM // 4
ROPE_THETA = 500000.0
CMP_BLOCK = 32
CMP_STRIDE = 16
CMP_RATIO = CMP_BLOCK // CMP_STRIDE
SLC_BLOCK = 64
N_SELECT = 16
WINDOW = 512
Q_BLOCK = 128
GMLP_WIDTH = BRANCH_WIDTH
GMLP_GROUPS = 8
CHUNK = 128
CONV_WIDTH = BRANCH_WIDTH
CONV_K = 31
PEER_HEADS = 8
PEER_KEYS = 128
PEER_QDIM = 256
PEER_TOPK = 16
PEER_BLOCK = 128
NORM_EPS = 1e-6
NEG_BIG = -1e30
FORCE_SCORE = 1e9

VMEM_LIMIT_BYTES = 56 * 1024 * 1024
MXU_DTYPE = jnp.bfloat16


def _mm_kernel(a_ref, b_ref, o_ref):
    o_ref[...] = jnp.dot(a_ref[...].astype(MXU_DTYPE), b_ref[...].astype(MXU_DTYPE),
                         preferred_element_type=jnp.float32)


def _pick_tm(m):
    for tm in (1024, 512, 256, 128):
        if m % tm == 0:
            return tm
    return m


def mm(a, b, tn=1024):
    m, k = a.shape
    _, n = b.shape
    tm = _pick_tm(m)
    tn = min(tn, n)
    if m > tm:
        a = a.astype(MXU_DTYPE)
    if jnp.dtype(b.dtype).itemsize > 2:
        tn = min(tn, 512)
    return pl.pallas_call(
        _mm_kernel,
        out_shape=jax.ShapeDtypeStruct((m, n), jnp.float32),
        grid=(m // tm, pl.cdiv(n, tn)),
        in_specs=[pl.BlockSpec((tm, k), lambda i, j: (i, 0)),
                  pl.BlockSpec((k, tn), lambda i, j: (0, j))],
        out_specs=pl.BlockSpec((tm, tn), lambda i, j: (i, j)),
        compiler_params=pltpu.CompilerParams(
            dimension_semantics=("parallel", "arbitrary"),
            vmem_limit_bytes=VMEM_LIMIT_BYTES),
        name="mm",
    )(a, b)


def mm3(x, w):
    b, t, k = x.shape
    return mm(x.reshape(b * t, k), w).reshape(b, t, w.shape[1])


def rms_norm(x, g):
    xf = x.astype(jnp.float32)
    y = xf * lax.rsqrt(jnp.mean(xf * xf, axis=-1, keepdims=True) + NORM_EPS)
    return (y * g.astype(jnp.float32)).astype(x.dtype)


def masked_softmax(s, mask):
    s = s.astype(jnp.float32)
    m = jnp.max(jnp.where(mask, s, NEG_BIG), axis=-1, keepdims=True)
    e = jnp.where(mask, jnp.exp(s - m), 0.0)
    return e / jnp.maximum(jnp.sum(e, axis=-1, keepdims=True), 1e-30)


def rope_partial(x, pos):
    half = ROT_DIM // 2
    freqs = ROPE_THETA ** (-jnp.arange(half, dtype=jnp.float32) / half)
    ang = pos.astype(jnp.float32)[:, None] * freqs[None, :]
    cos = jnp.cos(ang)[None, :, None, :]
    sin = jnp.sin(ang)[None, :, None, :]
    xr = x[..., :ROT_DIM].astype(jnp.float32)
    x1, x2 = xr[..., :half], xr[..., half:]
    rot = jnp.concatenate([x1 * cos - x2 * sin, x1 * sin + x2 * cos], axis=-1)
    return jnp.concatenate([rot.astype(x.dtype), x[..., ROT_DIM:]], axis=-1)


def adaln(c_prompt, c_sample, w_ada, b_ada):
    bp, bs = c_prompt.shape[0], c_sample.shape[0]
    rows = bp + bs
    pad = (-rows) % 8
    c_all = jnp.concatenate([c_prompt, c_sample, jnp.zeros((pad, c_prompt.shape[1]), c_prompt.dtype)], axis=0)
    mod = mm(jax.nn.silu(c_all), w_ada) + b_ada
    return jnp.split(mod[:bp], 6, axis=-1), jnp.split(mod[bp:rows], 6, axis=-1)


def modulate(x, g, shift, scale):
    return rms_norm(x, g) * (1.0 + scale[:, None, :]) + shift[:, None, :]


def mixer_front(x, shift, scale, norm_g, w_in, q_norm, k_norm, gmlp_norm, pos):
    b, t, _ = x.shape
    h = modulate(x, norm_g, shift, scale)
    z = mm3(h, w_in)
    sizes = [BRANCH_WIDTH, 6 * KV_WIDTH, 3 * N_HEADS, GMLP_WIDTH, GMLP_WIDTH, 2 * CONV_WIDTH]
    zq, zkv, zg, zu, zv, zglu = jnp.split(z, np.cumsum(sizes)[:-1].tolist(), axis=-1)
    q = rms_norm(zq.reshape(b, t, N_HEADS, HEAD_DIM), q_norm)
    q_n = q.reshape(b, t, KV_HEADS, Q_PER_KV, HEAD_DIM)
    q_r = rope_partial(q, pos).reshape(b, t, KV_HEADS, Q_PER_KV, HEAD_DIM)
    kv = zkv.reshape(b, t, 3, 2, KV_HEADS, HEAD_DIM)
    cmp_kv = kv[:, :, 0]
    ks = rope_partial(rms_norm(kv[:, :, 1, 0], k_norm[1]), pos)
    slc_kv = jnp.stack([ks, kv[:, :, 1, 1]], axis=2)
    kw = rope_partial(rms_norm(kv[:, :, 2, 0], k_norm[2]), pos)
    win_kv = jnp.stack([kw, kv[:, :, 2, 1]], axis=2)
    gates = jax.nn.sigmoid(zg.reshape(b, t, KV_HEADS, Q_PER_KV, 3))
    u = jax.nn.gelu(zu)
    v = rms_norm(jax.nn.gelu(zv), gmlp_norm)
    return h, q_r, q_n, gates, zg, cmp_kv, slc_kv, win_kv, u, v, zglu


def compress(cmp_kv, pe, w1, b1, w2, kn):
    b, L = cmp_kv.shape[:2]
    n_seg = L // CMP_STRIDE
    nb = n_seg - CMP_RATIO + 1
    seg = cmp_kv[:, :n_seg * CMP_STRIDE].reshape(b, n_seg, CMP_STRIDE, 2, KV_HEADS, HEAD_DIM)
    w1r = w1.reshape(2, CMP_RATIO, CMP_STRIDE, HEAD_DIM, HEAD_DIM)
    acc = (jnp.einsum('cld,cldh->ch', pe, w1) + b1)[:, None, :]
    for r in range(CMP_RATIO):
        acc = acc + jnp.einsum('bnlcgd,cldh->bncgh', seg[:, r:r + nb], w1r[:, r])
    out = jnp.einsum('bncgh,chd->bncgd', jax.nn.gelu(acc), w2)
    kc = rms_norm(out[:, :, 0], kn)
    vc = out[:, :, 1]
    c_end = jnp.asarray(np.arange(nb) * CMP_STRIDE + CMP_BLOCK - 1, dtype=jnp.int32)
    return kc, vc, c_end


def slc_blocks(slc_kv):
    b, L = slc_kv.shape[:2]
    ns = -(-L // SLC_BLOCK)
    kv = jnp.pad(slc_kv, ((0, 0), (0, ns * SLC_BLOCK - L), (0, 0), (0, 0), (0, 0)))
    kv = kv.reshape(b, ns, SLC_BLOCK, 2, KV_HEADS, HEAD_DIM).transpose(3, 0, 4, 1, 2, 5)
    return kv[0], kv[1]


def overlap_matrix_np(nb, ns):
    cs = np.arange(nb) * CMP_STRIDE
    ce = cs + CMP_BLOCK - 1
    ss = np.arange(ns) * SLC_BLOCK
    se = ss + SLC_BLOCK - 1
    return ((cs[:, None] <= se[None, :]) & (ce[:, None] >= ss[None, :])).astype(np.float32)


def overlap_matrix(nb, ns):
    return jnp.asarray(overlap_matrix_np(nb, ns))


def nsa_attend(q_r, q_n, gates, pos_q, kc, vc, c_end, ovl, sk, sv, wk, wv, w_pos):
    b, tq = q_r.shape[:2]
    scale = HEAD_DIM ** -0.5
    s = jnp.einsum('bqgrd,bngd->bqgrn', q_n, kc) * scale
    p = masked_softmax(s, (c_end[None, :] <= pos_q[:, None])[None, :, None, None, :])
    o_c = jnp.einsum('bqgrn,bngd->bqgrd', p.astype(vc.dtype), vc)
    imp = jnp.einsum('bqgrn,ns->bgqs', p, ovl)
    ns = ovl.shape[1]
    j = jnp.arange(ns, dtype=jnp.int32)[None, :]
    cur = (pos_q // SLC_BLOCK)[:, None]
    valid = j <= cur
    forced = (j == 0) | (j == cur) | (j == cur - 1)
    imp = jnp.where(valid[None, None], jnp.where(forced[None, None], FORCE_SCORE, imp), -FORCE_SCORE)
    n_sel = min(N_SELECT, ns)
    sel = lax.top_k(imp, n_sel)[1]
    gather = jax.vmap(jax.vmap(lambda kb, si: kb[si]))
    ksel = gather(sk, sel)
    vsel = gather(sv, sel)
    tok = sel[..., None] * SLC_BLOCK + jnp.arange(SLC_BLOCK, dtype=jnp.int32)
    ms = (tok <= pos_q[None, None, :, None, None]).transpose(0, 2, 1, 3, 4).reshape(b, tq, KV_HEADS, 1, n_sel * SLC_BLOCK)
    s2 = jnp.einsum('bqgrd,bgqnld->bqgrnl', q_r, ksel).reshape(b, tq, KV_HEADS, Q_PER_KV, n_sel * SLC_BLOCK) * scale
    p2 = masked_softmax(s2, ms)
    o_s = jnp.einsum('bqgrk,bgqkd->bqgrd', p2.astype(vsel.dtype), vsel.reshape(b, KV_HEADS, tq, n_sel * SLC_BLOCK, HEAD_DIM))
    s3 = jnp.einsum('bqgrd,bkgd->bqgrk', q_r, wk) * scale
    dlt = pos_q[:, None] - w_pos[None, :]
    mw = (dlt >= 0) & (dlt < WINDOW) & (w_pos[None, :] >= 0)
    p3 = masked_softmax(s3, mw[None, :, None, None, :])
    o_w = jnp.einsum('bqgrk,bkgd->bqgrd', p3.astype(wv.dtype), wv)
    return gates[..., 0:1] * o_c + gates[..., 1:2] * o_s + gates[..., 2:3] * o_w


NSA_KT = 1024
_NT_DIMS = (((1,), (1,)), ((), ()))


def _softmax_tile_update(s, mask, v, sl, m_scr, l_scr, acc_scr):
    m_old = m_scr[sl]
    m_new = jnp.maximum(m_old, jnp.max(jnp.where(mask, s, NEG_BIG), axis=-1, keepdims=True))
    alpha = jnp.exp(m_old - m_new)
    e = jnp.where(mask, jnp.exp(s - m_new), 0.0)
    l_scr[sl] = alpha * l_scr[sl] + jnp.sum(e, axis=-1, keepdims=True)
    acc_scr[sl] = alpha * acc_scr[sl] + jnp.dot(e.astype(MXU_DTYPE), v, preferred_element_type=jnp.float32)
    m_scr[sl] = m_new


def _nsa_prompt_kernel(qn_ref, qr_ref, gate_ref, kc_ref, vc_ref, ovl_ref, eexp_ref,
                       ks_ref, vs_ref, kw_ref, vw_ref, o_ref, m_scr, l_scr, acc_scr):
    qi = pl.program_id(2)
    q0 = qi * Q_BLOCK
    rows = Q_PER_KV * Q_BLOCK
    scale = HEAD_DIM ** -0.5
    qn = (qn_ref[0, 0].reshape(rows, HEAD_DIM) * scale).astype(MXU_DTYPE)
    qr = (qr_ref[0, 0].reshape(rows, HEAD_DIM) * scale).astype(MXU_DTYPE)
    pos_row = q0 + (lax.broadcasted_iota(jnp.int32, (rows, 1), 0) & (Q_BLOCK - 1))

    kc = kc_ref[0, 0]
    ncp = kc.shape[0]
    s = lax.dot_general(qn, kc, _NT_DIMS, preferred_element_type=jnp.float32)
    c_end = lax.broadcasted_iota(jnp.int32, (rows, ncp), 1) * CMP_STRIDE + (CMP_BLOCK - 1)
    cmask = c_end <= pos_row
    m = jnp.max(jnp.where(cmask, s, NEG_BIG), axis=-1, keepdims=True)
    e = jnp.where(cmask, jnp.exp(s - m), 0.0)
    p = e / jnp.maximum(jnp.sum(e, axis=-1, keepdims=True), 1e-30)
    o_c = jnp.dot(p.astype(MXU_DTYPE), vc_ref[0, 0], preferred_element_type=jnp.float32)

    psum = p[0:Q_BLOCK]
    for r in range(1, Q_PER_KV):
        psum = psum + p[r * Q_BLOCK:(r + 1) * Q_BLOCK]
    p_hi = psum.astype(MXU_DTYPE)
    p_lo = (psum - p_hi.astype(jnp.float32)).astype(MXU_DTYPE)
    ovl = ovl_ref[...]
    imp = (jnp.dot(p_hi, ovl, preferred_element_type=jnp.float32)
           + jnp.dot(p_lo, ovl, preferred_element_type=jnp.float32))
    ns = imp.shape[1]
    j = lax.broadcasted_iota(jnp.int32, (Q_BLOCK, ns), 1)
    cur = (q0 + lax.broadcasted_iota(jnp.int32, (Q_BLOCK, 1), 0)) // SLC_BLOCK
    forced = (j == 0) | (j == cur) | (j == cur - 1)
    adj = jnp.where(j <= cur, jnp.where(forced, FORCE_SCORE, imp), -FORCE_SCORE)
    rank = jnp.zeros((Q_BLOCK, ns), jnp.int32)
    for k in range(ns):
        col = adj[:, k:k + 1]
        beats = (col > adj) | ((col == adj) & (j > k))
        rank = rank + beats.astype(jnp.int32)
    sel = (rank < N_SELECT).astype(MXU_DTYPE)

    def init():
        m_scr[...] = jnp.full(m_scr.shape, NEG_BIG, jnp.float32)
        l_scr[...] = jnp.zeros(l_scr.shape, jnp.float32)
        acc_scr[...] = jnp.zeros(acc_scr.shape, jnp.float32)

    def finish():
        return acc_scr[...] / jnp.maximum(l_scr[...], 1e-30)

    init()
    pos_q = pos_row[0:Q_BLOCK]
    all_rows = slice(0, rows)

    def slc_body(kt, carry):
        k0 = pl.multiple_of(kt * NSA_KT, NSA_KT)
        s2 = lax.dot_general(qr, ks_ref[0, 0, pl.ds(k0, NSA_KT), :], _NT_DIMS,
                             preferred_element_type=jnp.float32)
        bm = jnp.dot(sel, eexp_ref[kt], preferred_element_type=jnp.float32)
        kpos = k0 + lax.broadcasted_iota(jnp.int32, (Q_BLOCK, NSA_KT), 1)
        mask = (bm > 0.5) & (kpos <= pos_q)
        mask = jnp.concatenate([mask] * Q_PER_KV, axis=0)
        _softmax_tile_update(s2, mask, vs_ref[0, 0, pl.ds(k0, NSA_KT), :], all_rows, m_scr, l_scr, acc_scr)
        return carry

    lax.fori_loop(0, (q0 + Q_BLOCK + NSA_KT - 1) // NSA_KT, slc_body, 0)
    o_s = finish()

    init()
    w0 = pl.multiple_of(jnp.maximum(q0 - WINDOW, 0), Q_BLOCK)
    span = WINDOW + Q_BLOCK
    s3 = lax.dot_general(qr, kw_ref[0, 0, pl.ds(w0, span), :], _NT_DIMS,
                         preferred_element_type=jnp.float32)
    dlt = pos_q - (w0 + lax.broadcasted_iota(jnp.int32, (Q_BLOCK, span), 1))
    wmask = (dlt >= 0) & (dlt < WINDOW)
    wmask = jnp.concatenate([wmask] * Q_PER_KV, axis=0)
    _softmax_tile_update(s3, wmask, vw_ref[0, 0, pl.ds(w0, span), :], all_rows, m_scr, l_scr, acc_scr)
    o_w = finish()

    gate = jax.nn.sigmoid(gate_ref[0, 0])
    outs = []
    for r in range(Q_PER_KV):
        sl = slice(r * Q_BLOCK, (r + 1) * Q_BLOCK)
        outs.append(gate[:, 3 * r:3 * r + 1] * o_c[sl] + gate[:, 3 * r + 1:3 * r + 2] * o_s[sl]
                    + gate[:, 3 * r + 2:3 * r + 3] * o_w[sl])
    o_ref[0] = jnp.concatenate(outs, axis=1).astype(o_ref.dtype)


def nsa_prompt(q_r, q_n, zg, kc, vc, slc_kv, win_kv):
    b, t = q_r.shape[:2]
    assert t % NSA_KT == 0 and t >= WINDOW + Q_BLOCK and WINDOW % Q_BLOCK == 0
    nb = kc.shape[1]
    ncp = -(-nb // 128) * 128
    ns = t // SLC_BLOCK
    heads_first = lambda a: a.transpose(0, 2, 3, 1, 4)
    kv_first = lambda a: a.transpose(0, 2, 1, 3).astype(MXU_DTYPE)
    pad_c = lambda a: jnp.pad(a, ((0, 0), (0, ncp - nb), (0, 0), (0, 0)))
    gate_in = zg.reshape(b, t, KV_HEADS, 3 * Q_PER_KV).transpose(0, 2, 1, 3)
    ovl = np.zeros((ncp, ns), np.float32)
    ovl[:nb] = np.asarray(overlap_matrix_np(nb, ns))
    key_blk = (np.arange(t) // SLC_BLOCK).reshape(t // NSA_KT, 1, NSA_KT)
    eexp = (key_blk == np.arange(ns).reshape(1, ns, 1)).astype(np.float32)
    rows = Q_PER_KV * Q_BLOCK
    q_spec = pl.BlockSpec((1, 1, Q_PER_KV, Q_BLOCK, HEAD_DIM), lambda bi, g, qi: (bi, g, 0, qi, 0))
    c_spec = pl.BlockSpec((1, 1, ncp, HEAD_DIM), lambda bi, g, qi: (bi, g, 0, 0))
    kv_spec = pl.BlockSpec((1, 1, t, HEAD_DIM), lambda bi, g, qi: (bi, g, 0, 0))
    return pl.pallas_call(
        _nsa_prompt_kernel,
        out_shape=jax.ShapeDtypeStruct((b, t, BRANCH_WIDTH), MXU_DTYPE),
        grid=(b, KV_HEADS, t // Q_BLOCK),
        in_specs=[q_spec, q_spec,
                  pl.BlockSpec((1, 1, Q_BLOCK, 3 * Q_PER_KV), lambda bi, g, qi: (bi, g, qi, 0)),
                  c_spec, c_spec,
                  pl.BlockSpec((ncp, ns), lambda bi, g, qi: (0, 0)),
                  pl.BlockSpec((t // NSA_KT, ns, NSA_KT), lambda bi, g, qi: (0, 0, 0)),
                  kv_spec, kv_spec, kv_spec, kv_spec],
        out_specs=pl.BlockSpec((1, Q_BLOCK, Q_PER_KV * HEAD_DIM), lambda bi, g, qi: (bi, qi, g)),
        scratch_shapes=[pltpu.VMEM((rows, 1), jnp.float32),
                        pltpu.VMEM((rows, 1), jnp.float32),
                        pltpu.VMEM((rows, HEAD_DIM), jnp.float32)],
        compiler_params=pltpu.CompilerParams(
            dimension_semantics=("parallel", "parallel", "arbitrary"),
            vmem_limit_bytes=VMEM_LIMIT_BYTES),
        name="nsa_prompt",
    )(heads_first(q_n), heads_first(q_r), gate_in,
      kv_first(pad_c(kc)), kv_first(pad_c(vc)),
      jnp.asarray(ovl, MXU_DTYPE), jnp.asarray(eexp, MXU_DTYPE),
      kv_first(slc_kv[:, :, 0]), kv_first(slc_kv[:, :, 1]),
      kv_first(win_kv[:, :, 0]), kv_first(win_kv[:, :, 1]))


SEGS_PER_PAGE = PAGE_SIZE // CMP_STRIDE
SEG_WIDTH = CMP_STRIDE * HEAD_DIM
GD = KV_HEADS * HEAD_DIM


def _masked_softmax_parts(s, mask, s_new, new_on):
    m = jnp.max(jnp.where(mask, s, NEG_BIG), axis=-1, keepdims=True)
    m = jnp.maximum(m, jnp.where(new_on, s_new, NEG_BIG))
    e = jnp.where(mask, jnp.exp(s - m), 0.0)
    e_new = jnp.where(new_on, jnp.exp(s_new - m), 0.0)
    den = jnp.maximum(jnp.sum(e, axis=-1, keepdims=True) + e_new, 1e-30)
    return e, e_new, den


def _nsa_sample_kernel(pt_ref, qn_ref, qr_ref, gate_ref, new_ref, w1_ref, b0_ref, w2_ref, kn_ref,
                       ovl_ref, eexp_ref, win_ref, *rest, n_pages, past_len, win_len):
    cmp_refs = rest[:n_pages]
    slc_refs = rest[n_pages:2 * n_pages]
    o_ref = rest[2 * n_pages]
    nh = N_HEADS
    n_seg = n_pages * SEGS_PER_PAGE
    row_group = lax.broadcasted_iota(jnp.int32, (nh, 1), 0) // Q_PER_KV

    outs = []
    for c in range(2):
        x = jnp.concatenate([cmp_refs[p][0, 0, c] for p in range(n_pages)], axis=0).astype(MXU_DTYPE)
        ab = jnp.dot(x, w1_ref[c], preferred_element_type=jnp.float32)
        rows_pp = KV_HEADS * SEGS_PER_PAGE
        ab = jnp.concatenate(
            [ab[p * rows_pp + g * SEGS_PER_PAGE:p * rows_pp + (g + 1) * SEGS_PER_PAGE]
             for g in range(KV_HEADS) for p in range(n_pages)], axis=0)
        a = ab[:, :HEAD_DIM]
        b_next = pltpu.roll(ab[:, HEAD_DIM:], KV_HEADS * n_seg - 1, axis=0)
        hid = jax.nn.gelu(a + b_next + b0_ref[c])
        outs.append(jnp.dot(hid.astype(MXU_DTYPE), w2_ref[c].astype(MXU_DTYPE),
                            preferred_element_type=jnp.float32))
    kc_all = outs[0]
    kc_all = kc_all * lax.rsqrt(jnp.mean(kc_all * kc_all, axis=-1, keepdims=True) + NORM_EPS) * kn_ref[...]
    vc_all = outs[1]

    qn = qn_ref[0].astype(MXU_DTYPE)
    s_c = jnp.zeros((nh, n_seg), jnp.float32)
    for g in range(KV_HEADS):
        kc_g = kc_all[g * n_seg:(g + 1) * n_seg].astype(MXU_DTYPE)
        s_g = lax.dot_general(qn, kc_g, _NT_DIMS, preferred_element_type=jnp.float32)
        s_c = jnp.where(row_group == g, s_g, s_c)
    blk = lax.broadcasted_iota(jnp.int32, (nh, n_seg), 1)
    cmask = (blk * CMP_STRIDE + (CMP_BLOCK - 1) <= past_len) & (blk < n_seg - CMP_RATIO + 1)
    m = jnp.max(jnp.where(cmask, s_c, NEG_BIG), axis=-1, keepdims=True)
    e = jnp.where(cmask, jnp.exp(s_c - m), 0.0)
    p = e / jnp.maximum(jnp.sum(e, axis=-1, keepdims=True), 1e-30)
    o_c = jnp.zeros((nh, HEAD_DIM), jnp.float32)
    p_bf = p.astype(MXU_DTYPE)
    psum = jnp.zeros((nh, n_seg), jnp.float32)
    for g in range(KV_HEADS):
        vc_g = vc_all[g * n_seg:(g + 1) * n_seg].astype(MXU_DTYPE)
        in_g = row_group == g
        o_c = jnp.where(in_g, jnp.dot(p_bf, vc_g, preferred_element_type=jnp.float32), o_c)
        psum = jnp.where(in_g, jnp.sum(jnp.where(in_g, p, 0.0), axis=0, keepdims=True), psum)

    p_hi = psum.astype(MXU_DTYPE)
    p_lo = (psum - p_hi.astype(jnp.float32)).astype(MXU_DTYPE)
    ovl = ovl_ref[...]
    imp = (jnp.dot(p_hi, ovl, preferred_element_type=jnp.float32)
           + jnp.dot(p_lo, ovl, preferred_element_type=jnp.float32))
    ns_pad = imp.shape[1]
    ns = past_len // SLC_BLOCK + 1
    cur = past_len // SLC_BLOCK
    j = lax.broadcasted_iota(jnp.int32, (nh, ns_pad), 1)
    forced = (j == 0) | (j == cur) | (j == cur - 1)
    adj = jnp.where(j <= cur, jnp.where(forced, FORCE_SCORE, imp), -FORCE_SCORE)
    adj = jnp.where(j < ns, adj, PEER_NEG)
    rank = jnp.zeros((nh, ns_pad), jnp.int32)
    for k in range(ns):
        col = adj[:, k:k + 1]
        beats = (col > adj) | ((col == adj) & (j > k))
        rank = rank + beats.astype(jnp.int32)
    sel = (rank < N_SELECT) & (j < ns)
    tok_mask = jnp.dot(sel.astype(MXU_DTYPE), eexp_ref[...], preferred_element_type=jnp.float32) > 0.5
    new_sel = jnp.sum(jnp.where(sel & (j == cur), 1.0, 0.0), axis=-1, keepdims=True) > 0.5

    qr = qr_ref[0]
    qr_bf = qr.astype(MXU_DTYPE)
    new = new_ref[0]
    to_mxu = lambda a: a.astype(MXU_DTYPE).astype(jnp.float32)
    s_new = jnp.sum(to_mxu(qr) * to_mxu(new[0:1]), axis=-1, keepdims=True)
    s_sel = jnp.concatenate(
        [jnp.dot(qr_bf, slc_refs[pg][0, 0, 0:GD, :].astype(MXU_DTYPE), preferred_element_type=jnp.float32)
         for pg in range(n_pages)], axis=1)
    e, e_new, den = _masked_softmax_parts(s_sel, tok_mask, s_new, new_sel)
    e_bf = e.astype(MXU_DTYPE)
    o_s = to_mxu(e_new) * to_mxu(new[1:2])
    for pg in range(n_pages):
        o_s = o_s + lax.dot_general(e_bf[:, pg * PAGE_SIZE:(pg + 1) * PAGE_SIZE],
                                    slc_refs[pg][0, 0, GD:2 * GD, :].astype(MXU_DTYPE), _NT_DIMS,
                                    preferred_element_type=jnp.float32)
    o_s = o_s / den

    sw_new = jnp.sum(to_mxu(qr) * to_mxu(new[2:3]), axis=-1, keepdims=True)
    s_w = jnp.dot(qr_bf, win_ref[0, 0, 0:GD, :].astype(MXU_DTYPE), preferred_element_type=jnp.float32)
    dlt = win_len - lax.broadcasted_iota(jnp.int32, (nh, win_len), 1)
    wmask = (dlt >= 0) & (dlt < WINDOW) & (past_len - dlt >= 0)
    e, e_new, den = _masked_softmax_parts(s_w, wmask, sw_new, jnp.full((nh, 1), True))
    o_w = (to_mxu(e_new) * to_mxu(new[3:4])
           + lax.dot_general(e.astype(MXU_DTYPE), win_ref[0, 0, GD:2 * GD, :].astype(MXU_DTYPE), _NT_DIMS,
                             preferred_element_type=jnp.float32)) / den

    def own_block(o_bd):
        out = jnp.zeros((nh, HEAD_DIM), jnp.float32)
        for g in range(KV_HEADS):
            out = jnp.where(row_group == g, o_bd[:, g * HEAD_DIM:(g + 1) * HEAD_DIM], out)
        return out

    gate = jax.nn.sigmoid(gate_ref[0])
    o_ref[0] = gate[:, 0:1] * o_c + gate[:, 1:2] * own_block(o_s) + gate[:, 2:3] * own_block(o_w)


def nsa_sample(q_r, q_n, zg, skv_new, wkv_new, cmp_pages, slc_pages, win_t, page_table, layer,
               cmp_pe, cmp_w1, cmp_b1, cmp_w2, kn):
    bs = q_r.shape[0]
    n_pages = page_table.shape[1]
    past_len = n_pages * PAGE_SIZE
    win_len = win_t.shape[-1]
    n_seg = n_pages * SEGS_PER_PAGE
    nb = n_seg - CMP_RATIO + 1
    ns = past_len // SLC_BLOCK + 1
    ns_pad = -(-ns // 128) * 128
    scale = HEAD_DIM ** -0.5
    qn = q_n.reshape(bs, N_HEADS, HEAD_DIM) * scale
    eye = jnp.eye(KV_HEADS, dtype=q_r.dtype)
    qr_bd = (q_r.reshape(bs, KV_HEADS, Q_PER_KV, 1, HEAD_DIM) * scale
             * eye[None, :, None, :, None]).reshape(bs, N_HEADS, GD)
    gate_in = zg.reshape(bs, N_HEADS, 3)
    new = jnp.concatenate([skv_new.reshape(bs, 2, GD), wkv_new.reshape(bs, 2, GD)], axis=1)
    w1r = cmp_w1.reshape(2, CMP_RATIO, SEG_WIDTH, HEAD_DIM)
    w1c = jnp.concatenate([w1r[:, r] for r in range(CMP_RATIO)], axis=-1).astype(MXU_DTYPE)
    b0 = (jnp.einsum('cld,cldh->ch', cmp_pe, cmp_w1, precision=lax.Precision.HIGHEST) + cmp_b1)[:, None, :]
    ovl = np.zeros((n_seg, ns_pad), np.float32)
    ovl[:nb, :ns] = overlap_matrix_np(nb, ns)
    eexp = (np.arange(past_len)[None, :] // SLC_BLOCK == np.arange(ns_pad)[:, None]).astype(np.float32)
    const = lambda shape: pl.BlockSpec(shape, lambda b, pt: (0,) * len(shape))
    per_b = lambda shape: pl.BlockSpec((1,) + shape, lambda b, pt: (b,) + (0,) * len(shape))
    cmp_spec = [pl.BlockSpec((1, 1, 2, KV_HEADS * SEGS_PER_PAGE, SEG_WIDTH),
                             functools.partial(lambda b, pt, pg: (layer, pt[b, pg], 0, 0, 0), pg=pg))
                for pg in range(n_pages)]
    slc_spec = [pl.BlockSpec((1, 1, 2 * GD, PAGE_SIZE),
                             functools.partial(lambda b, pt, pg: (layer, pt[b, pg], 0, 0), pg=pg))
                for pg in range(n_pages)]
    grid_spec = pltpu.PrefetchScalarGridSpec(
        num_scalar_prefetch=1,
        grid=(bs,),
        in_specs=[per_b((N_HEADS, HEAD_DIM)), per_b((N_HEADS, GD)), per_b((N_HEADS, 3)), per_b((4, GD)),
                  const((2, SEG_WIDTH, CMP_RATIO * HEAD_DIM)), const((2, 1, HEAD_DIM)),
                  const((2, HEAD_DIM, HEAD_DIM)), const((1, HEAD_DIM)),
                  const((n_seg, ns_pad)), const((ns_pad, past_len)),
                  pl.BlockSpec((1, 1, 2 * GD, win_len), lambda b, pt: (layer, b, 0, 0))]
                 + cmp_spec + slc_spec,
        out_specs=per_b((N_HEADS, HEAD_DIM)),
    )
    out = pl.pallas_call(
        functools.partial(_nsa_sample_kernel, n_pages=n_pages, past_len=past_len, win_len=win_len),
        out_shape=jax.ShapeDtypeStruct((bs, N_HEADS, HEAD_DIM), jnp.float32),
        grid_spec=grid_spec,
        compiler_params=pltpu.CompilerParams(
            dimension_semantics=("arbitrary",),
            vmem_limit_bytes=VMEM_LIMIT_BYTES),
        name="nsa_sample",
    )(page_table, qn, qr_bd, gate_in, new, w1c, b0, cmp_w2, kn.reshape(1, HEAD_DIM),
      jnp.asarray(ovl, MXU_DTYPE), jnp.asarray(eexp, MXU_DTYPE), win_t,
      *([cmp_pages] * n_pages), *([slc_pages] * n_pages))
    return out.reshape(bs, 1, BRANCH_WIDTH)


def gmlp_mix(u, v, w_s, b_s):
    b, t, _ = v.shape
    nc = -(-t // CHUNK)
    vp = jnp.pad(v, ((0, 0), (0, nc * CHUNK - t), (0, 0))).reshape(b, nc, CHUNK, GMLP_GROUPS, GMLP_WIDTH // GMLP_GROUPS)
    w = w_s * jnp.tril(jnp.ones((CHUNK, CHUNK), w_s.dtype))
    mixed = jnp.einsum('gpq,bcqgd->bcpgd', w, vp) + b_s.T[None, None, :, :, None]
    return u * mixed.reshape(b, nc * CHUNK, GMLP_WIDTH)[:, :t]


def conformer_conv(zglu, buf, w_dw, b_dw, norm_g):
    a, gt = jnp.split(zglu, 2, axis=-1)
    glu = a * jax.nn.sigmoid(gt)
    xin = jnp.concatenate([buf.astype(glu.dtype), glu], axis=1)
    y = lax.conv_general_dilated(xin, w_dw[:, None, :].astype(glu.dtype), (1,), 'VALID',
                                 dimension_numbers=('NWC', 'WIO', 'NWC'), feature_group_count=CONV_WIDTH) + b_dw
    y = jax.nn.silu(rms_norm(y, norm_g))
    return y, xin[:, xin.shape[1] - (CONV_K - 1):]


MERGE_TN = 512


def _merge_kernel(h_ref, o0_ref, o1_ref, o2_ref, wm0_ref, wm1_ref, wm2_ref, bm0_ref, bm1_ref, bm2_ref,
                  wb_ref, y_ref):
    h = h_ref[...].astype(MXU_DTYPE)
    acc = None
    for i, (o_ref, wm_ref, bm_ref) in enumerate(((o0_ref, wm0_ref, bm0_ref), (o1_ref, wm1_ref, bm1_ref),
                                                 (o2_ref, wm2_ref, bm2_ref))):
        mg = jax.nn.sigmoid(jnp.dot(h, wm_ref[...].astype(MXU_DTYPE), preferred_element_type=jnp.float32)
                            + bm_ref[...])
        term = mg * jnp.dot(o_ref[...].astype(MXU_DTYPE), wb_ref[i].astype(MXU_DTYPE),
                            preferred_element_type=jnp.float32)
        acc = term if acc is None else acc + term
    y_ref[...] = acc.astype(y_ref.dtype)


def merge_branches(h, o_nsa, o_gmlp, o_conv, w_branch, w_merge, b_merge):
    m, d = h.shape
    tm = 512 if m % 512 == 0 else m
    tn = MERGE_TN
    nj = d // tn
    bw = o_nsa.shape[1]
    b2 = b_merge.reshape(1, 3 * d)
    o_spec = pl.BlockSpec((tm, bw), lambda i, j: (i, 0))
    wm_specs = [pl.BlockSpec((d, tn), functools.partial(lambda i, j, k: (0, k * nj + j), k=k)) for k in range(3)]
    bm_specs = [pl.BlockSpec((1, tn), functools.partial(lambda i, j, k: (0, k * nj + j), k=k)) for k in range(3)]
    return pl.pallas_call(
        _merge_kernel,
        out_shape=jax.ShapeDtypeStruct((m, d), MXU_DTYPE),
        grid=(m // tm, nj),
        in_specs=[pl.BlockSpec((tm, d), lambda i, j: (i, 0)), o_spec, o_spec, o_spec]
                 + wm_specs + bm_specs + [pl.BlockSpec((3, bw, tn), lambda i, j: (0, 0, j))],
        out_specs=pl.BlockSpec((tm, tn), lambda i, j: (i, j)),
        compiler_params=pltpu.CompilerParams(
            dimension_semantics=("parallel", "arbitrary"),
            vmem_limit_bytes=VMEM_LIMIT_BYTES),
        name="merge_branches",
    )(h, o_nsa, o_gmlp, o_conv, w_merge, w_merge, w_merge, b2, b2, b2, w_branch)


def mixer_merge(x, h, o_nsa, o_gmlp, o_conv, gate, w_branch, w_merge, b_merge, w_out):
    b, t, d = x.shape
    flat = lambda a: a.reshape(b * t, a.shape[-1]).astype(MXU_DTYPE)
    y = merge_branches(flat(h), flat(o_nsa), flat(o_gmlp), flat(o_conv), w_branch, w_merge, b_merge)
    return x + gate[:, None, :] * mm(y, w_out).reshape(b, t, d)


PEER_TB = 512
PEER_I1 = 8
PEER_JSUB = 4
PEER_SLAB = 256
PEER_OUT_SLAB = 512
PEER_KSUB = 32
PEER_NEG = -3.0e38
PEER_NRANK = PEER_TOPK + 1
PEER_VROWS = 24
PEER_NCAND = PEER_VROWS + 7 * 8 + 16


def _extract_top(cur_ref, out_ref, n_out):
    n = cur_ref.shape[0]
    iota = lax.broadcasted_iota(jnp.int32, cur_ref.shape, 0)
    for k in range(n_out):
        cur = cur_ref[...]
        m = jnp.max(cur, axis=0, keepdims=True)
        first = jnp.min(jnp.where(cur == m, iota, n), axis=0, keepdims=True)
        cur_ref[...] = jnp.where(iota == first, PEER_NEG, cur)
        out_ref[k:k + 1, :] = m


def _peer_route_kernel(x_ref, w_ref, k_ref, thr1_ref, e1_ref, s2_ref, e2_ref,
                       work_ref, v1_ref, v2_ref, cand_ref, ctop_ref):
    x = x_ref[...].astype(MXU_DTYPE)
    w = w_ref[...].astype(MXU_DTYPE)
    q = jnp.dot(x, w, preferred_element_type=jnp.float32)
    half = PEER_QDIM // 2
    nt = (((1,), (1,)), ((), ()))
    s = []
    for c in range(2):
        qc = q[:, c * half:(c + 1) * half].astype(MXU_DTYPE)
        kc = k_ref[0, c].astype(MXU_DTYPE)
        s.append(lax.dot_general(kc, qc, nt, preferred_element_type=jnp.float32))
    s1, s2 = s
    v1_ref[...] = jnp.full(v1_ref.shape, PEER_NEG, jnp.float32)
    v2_ref[...] = jnp.full(v2_ref.shape, PEER_NEG, jnp.float32)
    work_ref[...] = s1
    _extract_top(work_ref, v1_ref, PEER_NRANK)
    work_ref[...] = s2
    _extract_top(work_ref, v2_ref, PEER_NRANK)
    v1 = v1_ref[...]
    v2 = v2_ref[...]
    nv = PEER_VROWS
    cand_ref[0:nv, :] = v1[0:1] + v2
    for a in range(1, 8):
        cand_ref[nv + 8 * (a - 1):nv + 8 * a, :] = v1[a:a + 1] + v2[0:8]
    cand_ref[nv + 56:nv + 72, :] = v1[8:nv] + v2[0:1]
    _extract_top(cand_ref, ctop_ref, PEER_NRANK)
    tau = 0.5 * (ctop_ref[PEER_TOPK - 1:PEER_TOPK, :] + ctop_ref[PEER_TOPK:PEER_TOPK + 1, :])
    m1 = v1[0:1]
    m2 = v2[0:1]
    e1top = jnp.exp(v1 - m1)
    e2top = jnp.exp(v2 - m2)
    z = jnp.zeros_like(tau)
    for a in range(PEER_NRANK):
        sel = v2 >= (tau - v1[a:a + 1])
        z = z + jnp.sum(jnp.where(sel, e2top, 0.0), axis=0, keepdims=True) * e1top[a:a + 1]
    inv_z = 1.0 / z
    thr1_ref[0] = tau - s1
    e1_ref[0] = jnp.exp(s1 - m1) * inv_z
    s2_ref[0] = s2
    e2_ref[0] = jnp.exp(s2 - m2)


def peer_route(xm, w_pq, sub_keys):
    t, d = xm.shape
    tb = PEER_TB
    out = jax.ShapeDtypeStruct((PEER_HEADS, PEER_KEYS, t), jnp.float32)
    ospec = pl.BlockSpec((1, PEER_KEYS, tb), lambda i, h: (h, 0, i))
    return pl.pallas_call(
        _peer_route_kernel,
        out_shape=(out, out, out, out),
        grid=(t // tb, PEER_HEADS),
        in_specs=[pl.BlockSpec((tb, d), lambda i, h: (i, 0)),
                  pl.BlockSpec((d, PEER_QDIM), lambda i, h: (0, h)),
                  pl.BlockSpec((1, 2, PEER_KEYS, PEER_QDIM // 2), lambda i, h: (h, 0, 0, 0))],
        out_specs=(ospec, ospec, ospec, ospec),
        scratch_shapes=[pltpu.VMEM((PEER_KEYS, tb), jnp.float32),
                        pltpu.VMEM((PEER_VROWS, tb), jnp.float32),
                        pltpu.VMEM((PEER_VROWS, tb), jnp.float32),
                        pltpu.VMEM((PEER_NCAND, tb), jnp.float32),
                        pltpu.VMEM((PEER_VROWS, tb), jnp.float32)],
        compiler_params=pltpu.CompilerParams(
            dimension_semantics=("parallel", "arbitrary"),
            vmem_limit_bytes=VMEM_LIMIT_BYTES),
        name="peer_route",
    )(xm, w_pq, sub_keys)


def _peer_dense_kernel(xt_ref, u_ref, vt_ref, thr1_ref, e1_ref, s2_ref, e2_ref, o_ref, g_scr, w_new):
    @pl.when(pl.program_id(1) == 0)
    def _():
        o_ref[...] = jnp.zeros_like(o_ref)

    tb = o_ref.shape[1]
    nk = PEER_KEYS
    ks = PEER_KSUB
    for t0 in range(0, tb, 128):
        lanes = slice(t0, t0 + 128)
        for k0 in range(0, nk, ks):
            for j0 in range(0, PEER_I1, PEER_JSUB):
                g = [jnp.zeros((ks, 128), jnp.float32) for _ in range(PEER_JSUB)]
                for h in range(PEER_HEADS):
                    s2 = s2_ref[h, k0:k0 + ks, lanes]
                    e2 = e2_ref[h, k0:k0 + ks, lanes]
                    for jj in range(PEER_JSUB):
                        j = j0 + jj
                        sel = s2 >= thr1_ref[h, j:j + 1, lanes]
                        g[jj] = g[jj] + jnp.where(sel, e2, 0.0) * e1_ref[h, j:j + 1, lanes]
                for jj in range(PEER_JSUB):
                    r0 = (j0 + jj) * nk + k0
                    g_scr[r0:r0 + ks, lanes] = g[jj]
    ec = u_ref.shape[0]
    for r0 in range(0, ec, PEER_SLAB):
        act = jax.nn.gelu(jnp.dot(u_ref[r0:r0 + PEER_SLAB, :], xt_ref[...], preferred_element_type=jnp.float32))
        w_new[r0:r0 + PEER_SLAB, :] = (act * g_scr[r0:r0 + PEER_SLAB, :]).astype(MXU_DTYPE)
    for r0 in range(0, o_ref.shape[0], PEER_OUT_SLAB):
        o_ref[r0:r0 + PEER_OUT_SLAB, :] += jnp.dot(vt_ref[r0:r0 + PEER_OUT_SLAB, :], w_new[...],
                                                   preferred_element_type=jnp.float32)


def peer_dense(xt, u_bf, vt_bf, thr1, e1, s2, e2):
    d, t = xt.shape
    n_exp = u_bf.shape[0]
    tb = PEER_TB
    ec = PEER_I1 * PEER_KEYS
    row_spec = pl.BlockSpec((PEER_HEADS, PEER_I1, tb), lambda i, c: (0, c, i))
    full_spec = pl.BlockSpec((PEER_HEADS, PEER_KEYS, tb), lambda i, c: (0, 0, i))
    return pl.pallas_call(
        _peer_dense_kernel,
        out_shape=jax.ShapeDtypeStruct((d, t), jnp.float32),
        grid=(t // tb, n_exp // ec),
        in_specs=[pl.BlockSpec((d, tb), lambda i, c: (0, i)),
                  pl.BlockSpec((ec, d), lambda i, c: (c, 0)),
                  pl.BlockSpec((d, ec), lambda i, c: (0, c)),
                  row_spec, row_spec, full_spec, full_spec],
        out_specs=pl.BlockSpec((d, tb), lambda i, c: (0, i)),
        scratch_shapes=[pltpu.VMEM((ec, tb), jnp.float32), pltpu.VMEM((ec, tb), MXU_DTYPE)],
        compiler_params=pltpu.CompilerParams(
            dimension_semantics=("parallel", "arbitrary"),
            vmem_limit_bytes=VMEM_LIMIT_BYTES),
        name="peer_dense",
    )(xt, u_bf, vt_bf, thr1, e1, s2, e2)


def peer_ffn(xm, w_pq, sub_keys, u_bf, vt_bf):
    t, d = xm.shape
    tp = -(-t // PEER_TB) * PEER_TB
    xm_p = jnp.pad(xm, ((0, tp - t), (0, 0)))
    thr1, e1, s2, e2 = peer_route(xm_p, w_pq, sub_keys)
    out_t = peer_dense(xm_p.T.astype(MXU_DTYPE), u_bf, vt_bf, thr1, e1, s2, e2)
    return out_t.T[:t]


def kernel(x_prompt, x_sample, cache_cmp_kv, cache_slc_kv, state_win_kv, state_conv, page_table,
           c_prompt, c_sample, w_ada, b_ada, norm_mix, norm_ffn, w_in, q_norm, k_norm,
           cmp_pe, cmp_w1, cmp_b1, cmp_w2, gmlp_norm, gmlp_ws, gmlp_bs, conv_w, conv_b, conv_norm,
           w_branch, w_merge, b_merge, w_out, peer_wq, peer_keys, peer_u, peer_v):
    bp, t = x_prompt.shape[:2]
    bs, ds = x_sample.shape[:2]
    depth = w_in.shape[0]
    past_len = page_table.shape[1] * PAGE_SIZE
    pos_p = jnp.arange(t, dtype=jnp.int32)
    pos_s = past_len + jnp.arange(ds, dtype=jnp.int32)
    xp, xs = x_prompt, x_sample
    assert ds == 1
    n_phys = cache_cmp_kv.shape[1]
    cmp_pages = cache_cmp_kv.reshape(depth, n_phys, SEGS_PER_PAGE, CMP_STRIDE, 2, KV_HEADS, HEAD_DIM)
    cmp_pages = cmp_pages.transpose(0, 1, 4, 5, 2, 3, 6).reshape(
        depth, n_phys, 2, KV_HEADS * SEGS_PER_PAGE, SEG_WIDTH).astype(MXU_DTYPE)
    slc_pages = cache_slc_kv.transpose(0, 1, 3, 4, 5, 2).reshape(depth, n_phys, 2 * GD, PAGE_SIZE)
    win_t = state_win_kv.transpose(0, 1, 3, 4, 5, 2).reshape(depth, bs, 2 * GD, state_win_kv.shape[2])
    cmp_p, cmp_s, slc_p, slc_s, win_p, win_s, conv_p, conv_s, gv_s = [], [], [], [], [], [], [], [], []
    for l in range(depth):
        mp, msm = adaln(c_prompt, c_sample, w_ada[l], b_ada[l])
        w_in_l, w_merge_l, w_branch_l, w_out_l = (w.astype(MXU_DTYPE) for w in
                                                  (w_in[l], w_merge[l], w_branch[l], w_out[l]))
        hp, q_r, q_n, gates, zg, ckv, skv, wkv, u, v, zglu = mixer_front(
            xp, mp[0], mp[1], norm_mix[l], w_in_l, q_norm[l], k_norm[l], gmlp_norm[l], pos_p)
        kc, vc, c_end = compress(ckv, cmp_pe[l], cmp_w1[l], cmp_b1[l], cmp_w2[l], k_norm[l, 0])
        o_nsa = nsa_prompt(q_r, q_n, zg, kc, vc, skv, wkv)
        o_g = gmlp_mix(u, v, gmlp_ws[l], gmlp_bs[l])
        o_c, buf = conformer_conv(zglu, jnp.zeros((bp, CONV_K - 1, CONV_WIDTH), zglu.dtype),
                                  conv_w[l], conv_b[l], conv_norm[l])
        xp = mixer_merge(xp, hp, o_nsa, o_g, o_c, mp[2], w_branch_l, w_merge_l, b_merge[l], w_out_l)
        cmp_p.append(ckv)
        slc_p.append(skv)
        win_p.append(wkv[:, t - min(WINDOW, t):])
        conv_p.append(buf)
        hs, q_r, q_n, gates, zg, ckv, skv, wkv, u, v, zglu = mixer_front(
            xs, msm[0], msm[1], norm_mix[l], w_in_l, q_norm[l], k_norm[l], gmlp_norm[l], pos_s)
        win_full = jnp.concatenate([state_win_kv[l], wkv], axis=1)
        o_nsa = nsa_sample(q_r, q_n, zg, skv, wkv, cmp_pages, slc_pages, win_t, page_table, l,
                           cmp_pe[l], cmp_w1[l], cmp_b1[l], cmp_w2[l], k_norm[l, 0])
        o_g = gmlp_mix(u, v, gmlp_ws[l], gmlp_bs[l])
        o_c, buf = conformer_conv(zglu, state_conv[l], conv_w[l], conv_b[l], conv_norm[l])
        xs = mixer_merge(xs, hs, o_nsa, o_g, o_c, msm[2], w_branch_l, w_merge_l, b_merge[l], w_out_l)
        d = xp.shape[-1]
        xm_all = jnp.concatenate([modulate(xp, norm_ffn[l], mp[3], mp[4]).reshape(bp * t, d),
                                  modulate(xs, norm_ffn[l], msm[3], msm[4]).reshape(bs * ds, d)], axis=0)
        ffn = peer_ffn(xm_all, peer_wq[l].astype(MXU_DTYPE), peer_keys[l],
                       peer_u[l].astype(MXU_DTYPE), peer_v[l].T.astype(MXU_DTYPE))
        xp = xp + mp[5][:, None, :] * ffn[:bp * t].reshape(bp, t, d)
        xs = xs + msm[5][:, None, :] * ffn[bp * t:].reshape(bs, ds, d)
        cmp_s.append(ckv)
        slc_s.append(skv)
        lw = win_full.shape[1]
        win_s.append(win_full[:, lw - min(WINDOW, lw):])
        conv_s.append(buf)
        gv_s.append(v)
    return (xp, xs, jnp.stack(cmp_p), jnp.stack(cmp_s), jnp.stack(slc_p), jnp.stack(slc_s),
            jnp.stack(win_p), jnp.stack(win_s), jnp.stack(conv_p), jnp.stack(conv_s), jnp.stack(gv_s))
```

```python
import functools

import jax
import jax.numpy as jnp
import numpy as np
from jax import lax
from jax.experimental import pallas as pl
from jax.experimental.pallas import tpu as pltpu

D_MODEL = 2048
PAGE_SIZE = 128
BRANCH_WIDTH = D_MODEL // 2
HEAD_DIM = 64
N_HEADS = BRANCH_WIDTH // HEAD_DIM
KV_HEADS = 4
Q_PER_KV = N_HEADS // KV_HEADS
KV_WIDTH = KV_HEADS * HEAD_DIM
ROT_DIM = HEAD_DIM // 4
ROPE_THETA = 500000.0
CMP_BLOCK = 32
CMP_STRIDE = 16
CMP_RATIO = CMP_BLOCK // CMP_STRIDE
SLC_BLOCK = 64
N_SELECT = 16
WINDOW = 512
Q_BLOCK = 128
GMLP_WIDTH = BRANCH_WIDTH
GMLP_GROUPS = 8
CHUNK = 128
CONV_WIDTH = BRANCH_WIDTH
CONV_K = 31
PEER_HEADS = 8
PEER_KEYS = 128
PEER_QDIM = 256
PEER_TOPK = 16
PEER_BLOCK = 128
NORM_EPS = 1e-6
NEG_BIG = -1e30
FORCE_SCORE = 1e9

VMEM_LIMIT_BYTES = 56 * 1024 * 1024
MXU_DTYPE = jnp.bfloat16


def _mm_kernel(a_ref, b_ref, o_ref):
    o_ref[...] = jnp.dot(a_ref[...].astype(MXU_DTYPE), b_ref[...].astype(MXU_DTYPE),
                         preferred_element_type=jnp.float32)


def _pick_tm(m):
    for tm in (1024, 512, 256, 128):
        if m % tm == 0:
            return tm
    return m


def mm(a, b, tn=1024):
    m, k = a.shape
    _, n = b.shape
    tm = _pick_tm(m)
    tn = min(tn, n)
    if m > tm:
        a = a.astype(MXU_DTYPE)
    if jnp.dtype(b.dtype).itemsize > 2:
        tn = min(tn, 512)
    return pl.pallas_call(
        _mm_kernel,
        out_shape=jax.ShapeDtypeStruct((m, n), jnp.float32),
        grid=(m // tm, pl.cdiv(n, tn)),
        in_specs=[pl.BlockSpec((tm, k), lambda i, j: (i, 0)),
                  pl.BlockSpec((k, tn), lambda i, j: (0, j))],
        out_specs=pl.BlockSpec((tm, tn), lambda i, j: (i, j)),
        compiler_params=pltpu.CompilerParams(
            dimension_semantics=("parallel", "arbitrary"),
            vmem_limit_bytes=VMEM_LIMIT_BYTES),
        name="mm",
    )(a, b)


def mm3(x, w):
    b, t, k = x.shape
    return mm(x.reshape(b * t, k), w).reshape(b, t, w.shape[1])


def rms_norm(x, g):
    xf = x.astype(jnp.float32)
    y = xf * lax.rsqrt(jnp.mean(xf * xf, axis=-1, keepdims=True) + NORM_EPS)
    return (y * g.astype(jnp.float32)).astype(x.dtype)


def masked_softmax(s, mask):
    s = s.astype(jnp.float32)
    m = jnp.max(jnp.where(mask, s, NEG_BIG), axis=-1, keepdims=True)
    e = jnp.where(mask, jnp.exp(s - m), 0.0)
    return e / jnp.maximum(jnp.sum(e, axis=-1, keepdims=True), 1e-30)


def rope_partial(x, pos):
    half = ROT_DIM // 2
    freqs = ROPE_THETA ** (-jnp.arange(half, dtype=jnp.float32) / half)
    ang = pos.astype(jnp.float32)[:, None] * freqs[None, :]
    cos = jnp.cos(ang)[None, :, None, :]
    sin = jnp.sin(ang)[None, :, None, :]
    xr = x[..., :ROT_DIM].astype(jnp.float32)
    x1, x2 = xr[..., :half], xr[..., half:]
    rot = jnp.concatenate([x1 * cos - x2 * sin, x1 * sin + x2 * cos], axis=-1)
    return jnp.concatenate([rot.astype(x.dtype), x[..., ROT_DIM:]], axis=-1)


def adaln(c_prompt, c_sample, w_ada, b_ada):
    bp, bs = c_prompt.shape[0], c_sample.shape[0]
    rows = bp + bs
    pad = (-rows) % 8
    c_all = jnp.concatenate([c_prompt, c_sample, jnp.zeros((pad, c_prompt.shape[1]), c_prompt.dtype)], axis=0)
    mod = mm(jax.nn.silu(c_all), w_ada) + b_ada
    return jnp.split(mod[:bp], 6, axis=-1), jnp.split(mod[bp:rows], 6, axis=-1)


def modulate(x, g, shift, scale):
    return rms_norm(x, g) * (1.0 + scale[:, None, :]) + shift[:, None, :]


def mixer_front(x, shift, scale, norm_g, w_in, q_norm, k_norm, gmlp_norm, pos):
    b, t, _ = x.shape
    h = modulate(x, norm_g, shift, scale)
    z = mm3(h, w_in)
    sizes = [BRANCH_WIDTH, 6 * KV_WIDTH, 3 * N_HEADS, GMLP_WIDTH, GMLP_WIDTH, 2 * CONV_WIDTH]
    zq, zkv, zg, zu, zv, zglu = jnp.split(z, np.cumsum(sizes)[:-1].tolist(), axis=-1)
    q = rms_norm(zq.reshape(b, t, N_HEADS, HEAD_DIM), q_norm)
    q_n = q.reshape(b, t, KV_HEADS, Q_PER_KV, HEAD_DIM)
    q_r = rope_partial(q, pos).reshape(b, t, KV_HEADS, Q_PER_KV, HEAD_DIM)
    kv = zkv.reshape(b, t, 3, 2, KV_HEADS, HEAD_DIM)
    cmp_kv = kv[:, :, 0]
    ks = rope_partial(rms_norm(kv[:, :, 1, 0], k_norm[1]), pos)
    slc_kv = jnp.stack([ks, kv[:, :, 1, 1]], axis=2)
    kw = rope_partial(rms_norm(kv[:, :, 2, 0], k_norm[2]), pos)
    win_kv = jnp.stack([kw, kv[:, :, 2, 1]], axis=2)
    gates = jax.nn.sigmoid(zg.reshape(b, t, KV_HEADS, Q_PER_KV, 3))
    u = jax.nn.gelu(zu)
    v = rms_norm(jax.nn.gelu(zv), gmlp_norm)
    return h, q_r, q_n, gates, zg, cmp_kv, slc_kv, win_kv, u, v, zglu


def compress(cmp_kv, pe, w1, b1, w2, kn):
    b, L = cmp_kv.shape[:2]
    n_seg = L // CMP_STRIDE
    nb = n_seg - CMP_RATIO + 1
    seg = cmp_kv[:, :n_seg * CMP_STRIDE].reshape(b, n_seg, CMP_STRIDE, 2, KV_HEADS, HEAD_DIM)
    w1r = w1.reshape(2, CMP_RATIO, CMP_STRIDE, HEAD_DIM, HEAD_DIM)
    acc = (jnp.einsum('cld,cldh->ch', pe, w1) + b1)[:, None, :]
    for r in range(CMP_RATIO):
        acc = acc + jnp.einsum('bnlcgd,cldh->bncgh', seg[:, r:r + nb], w1r[:, r])
    out = jnp.einsum('bncgh,chd->bncgd', jax.nn.gelu(acc), w2)
    kc = rms_norm(out[:, :, 0], kn)
    vc = out[:, :, 1]
    c_end = jnp.asarray(np.arange(nb) * CMP_STRIDE + CMP_BLOCK - 1, dtype=jnp.int32)
    return kc, vc, c_end


def slc_blocks(slc_kv):
    b, L = slc_kv.shape[:2]
    ns = -(-L // SLC_BLOCK)
    kv = jnp.pad(slc_kv, ((0, 0), (0, ns * SLC_BLOCK - L), (0, 0), (0, 0), (0, 0)))
    kv = kv.reshape(b, ns, SLC_BLOCK, 2, KV_HEADS, HEAD_DIM).transpose(3, 0, 4, 1, 2, 5)
    return kv[0], kv[1]


def overlap_matrix_np(nb, ns):
    cs = np.arange(nb) * CMP_STRIDE
    ce = cs + CMP_BLOCK - 1
    ss = np.arange(ns) * SLC_BLOCK
    se = ss + SLC_BLOCK - 1
    return ((cs[:, None] <= se[None, :]) & (ce[:, None] >= ss[None, :])).astype(np.float32)


def overlap_matrix(nb, ns):
    return jnp.asarray(overlap_matrix_np(nb, ns))


def nsa_attend(q_r, q_n, gates, pos_q, kc, vc, c_end, ovl, sk, sv, wk, wv, w_pos):
    b, tq = q_r.shape[:2]
    scale = HEAD_DIM ** -0.5
    s = jnp.einsum('bqgrd,bngd->bqgrn', q_n, kc) * scale
    p = masked_softmax(s, (c_end[None, :] <= pos_q[:, None])[None, :, None, None, :])
    o_c = jnp.einsum('bqgrn,bngd->bqgrd', p.astype(vc.dtype), vc)
    imp = jnp.einsum('bqgrn,ns->bgqs', p, ovl)
    ns = ovl.shape[1]
    j = jnp.arange(ns, dtype=jnp.int32)[None, :]
    cur = (pos_q // SLC_BLOCK)[:, None]
    valid = j <= cur
    forced = (j == 0) | (j == cur) | (j == cur - 1)
    imp = jnp.where(valid[None, None], jnp.where(forced[None, None], FORCE_SCORE, imp), -FORCE_SCORE)
    n_sel = min(N_SELECT, ns)
    sel = lax.top_k(imp, n_sel)[1]
    gather = jax.vmap(jax.vmap(lambda kb, si: kb[si]))
    ksel = gather(sk, sel)
    vsel = gather(sv, sel)
    tok = sel[..., None] * SLC_BLOCK + jnp.arange(SLC_BLOCK, dtype=jnp.int32)
    ms = (tok <= pos_q[None, None, :, None, None]).transpose(0, 2, 1, 3, 4).reshape(b, tq, KV_HEADS, 1, n_sel * SLC_BLOCK)
    s2 = jnp.einsum('bqgrd,bgqnld->bqgrnl', q_r, ksel).reshape(b, tq, KV_HEADS, Q_PER_KV, n_sel * SLC_BLOCK) * scale
    p2 = masked_softmax(s2, ms)
    o_s = jnp.einsum('bqgrk,bgqkd->bqgrd', p2.astype(vsel.dtype), vsel.reshape(b, KV_HEADS, tq, n_sel * SLC_BLOCK, HEAD_DIM))
    s3 = jnp.einsum('bqgrd,bkgd->bqgrk', q_r, wk) * scale
    dlt = pos_q[:, None] - w_pos[None, :]
    mw = (dlt >= 0) & (dlt < WINDOW) & (w_pos[None, :] >= 0)
    p3 = masked_softmax(s3, mw[None, :, None, None, :])
    o_w = jnp.einsum('bqgrk,bkgd->bqgrd', p3.astype(wv.dtype), wv)
    return gates[..., 0:1] * o_c + gates[..., 1:2] * o_s + gates[..., 2:3] * o_w


NSA_KT = 1024
_NT_DIMS = (((1,), (1,)), ((), ()))


def _softmax_tile_update(s, mask, v, sl, m_scr, l_scr, acc_scr):
    m_old = m_scr[sl]
    m_new = jnp.maximum(m_old, jnp.max(jnp.where(mask, s, NEG_BIG), axis=-1, keepdims=True))
    alpha = jnp.exp(m_old - m_new)
    e = jnp.where(mask, jnp.exp(s - m_new), 0.0)
    l_scr[sl] = alpha * l_scr[sl] + jnp.sum(e, axis=-1, keepdims=True)
    acc_scr[sl] = alpha * acc_scr[sl] + jnp.dot(e.astype(MXU_DTYPE), v, preferred_element_type=jnp.float32)
    m_scr[sl] = m_new


def _nsa_prompt_kernel(qn_ref, qr_ref, gate_ref, kc_ref, vc_ref, ovl_ref, eexp_ref,
                       ks_ref, vs_ref, kw_ref, vw_ref, o_ref, m_scr, l_scr, acc_scr, *, ns):
    qi = pl.program_id(2)
    q0 = qi * Q_BLOCK
    rows = Q_PER_KV * Q_BLOCK
    scale = HEAD_DIM ** -0.5
    qn = (qn_ref[0, 0].reshape(rows, HEAD_DIM) * scale).astype(MXU_DTYPE)
    qr = (qr_ref[0, 0].reshape(rows, HEAD_DIM) * scale).astype(MXU_DTYPE)
    pos_row = q0 + (lax.broadcasted_iota(jnp.int32, (rows, 1), 0) & (Q_BLOCK - 1))

    kc = kc_ref[0, 0]
    ncp = kc.shape[0]
    s = lax.dot_general(qn, kc, _NT_DIMS, preferred_element_type=jnp.float32)
    c_end = lax.broadcasted_iota(jnp.int32, (rows, ncp), 1) * CMP_STRIDE + (CMP_BLOCK - 1)
    cmask = c_end <= pos_row
    m = jnp.max(jnp.where(cmask, s, NEG_BIG), axis=-1, keepdims=True)
    e = jnp.where(cmask, jnp.exp(s - m), 0.0)
    p = e / jnp.maximum(jnp.sum(e, axis=-1, keepdims=True), 1e-30)
    o_c = jnp.dot(p.astype(MXU_DTYPE), vc_ref[0, 0], preferred_element_type=jnp.float32)

    psum = p[0:Q_BLOCK]
    for r in range(1, Q_PER_KV):
        psum = psum + p[r * Q_BLOCK:(r + 1) * Q_BLOCK]
    p_hi = psum.astype(MXU_DTYPE)
    p_lo = (psum - p_hi.astype(jnp.float32)).astype(MXU_DTYPE)
    ovl = ovl_ref[...]
    imp = (jnp.dot(p_hi, ovl, preferred_element_type=jnp.float32)
           + jnp.dot(p_lo, ovl, preferred_element_type=jnp.float32))
    ns_pad = imp.shape[1]
    j = lax.broadcasted_iota(jnp.int32, (Q_BLOCK, ns_pad), 1)
    cur = (q0 + lax.broadcasted_iota(jnp.int32, (Q_BLOCK, 1), 0)) // SLC_BLOCK
    forced = (j == 0) | (j == cur) | (j == cur - 1)
    adj = jnp.where(j <= cur, jnp.where(forced, FORCE_SCORE, imp), -FORCE_SCORE)
    a = adj.T[0:ns]
    jrow = lax.broadcasted_iota(jnp.int32, (ns, Q_BLOCK), 0)
    rank = jnp.zeros((ns, Q_BLOCK), jnp.int32)
    for k in range(ns):
        row = a[k:k + 1, :]
        beats = (row > a) | ((row == a) & (jrow > k))
        rank = rank + beats.astype(jnp.int32)
    sel_t = (rank < N_SELECT).astype(jnp.float32)
    if ns_pad > ns:
        sel_t = jnp.concatenate([sel_t, jnp.zeros((ns_pad - ns, Q_BLOCK), jnp.float32)], axis=0)
    sel = sel_t.T.astype(MXU_DTYPE)

    def init():
        m_scr[...] = jnp.full(m_scr.shape, NEG_BIG, jnp.float32)
        l_scr[...] = jnp.zeros(l_scr.shape, jnp.float32)
        acc_scr[...] = jnp.zeros(acc_scr.shape, jnp.float32)

    def finish():
        return acc_scr[...] / jnp.maximum(l_scr[...], 1e-30)

    init()
    pos_q = pos_row[0:Q_BLOCK]
    all_rows = slice(0, rows)

    def slc_body(kt, carry):
        k0 = pl.multiple_of(kt * NSA_KT, NSA_KT)
        s2 = lax.dot_general(qr, ks_ref[0, 0, pl.ds(k0, NSA_KT), :], _NT_DIMS,
                             preferred_element_type=jnp.float32)
        bm = jnp.dot(sel, eexp_ref[kt], preferred_element_type=jnp.float32)
        kpos = k0 + lax.broadcasted_iota(jnp.int32, (Q_BLOCK, NSA_KT), 1)
        mask = (bm > 0.5) & (kpos <= pos_q)
        mask = jnp.concatenate([mask] * Q_PER_KV, axis=0)
        _softmax_tile_update(s2, mask, vs_ref[0, 0, pl.ds(k0, NSA_KT), :], all_rows, m_scr, l_scr, acc_scr)
        return carry

    lax.fori_loop(0, (q0 + Q_BLOCK + NSA_KT - 1) // NSA_KT, slc_body, 0)
    o_s = finish()

    init()
    w0 = pl.multiple_of(jnp.maximum(q0 - WINDOW, 0), Q_BLOCK)
    span = WINDOW + Q_BLOCK
    s3 = lax.dot_general(qr, kw_ref[0, 0, pl.ds(w0, span), :], _NT_DIMS,
                         preferred_element_type=jnp.float32)
    dlt = pos_q - (w0 + lax.broadcasted_iota(jnp.int32, (Q_BLOCK, span), 1))
    wmask = (dlt >= 0) & (dlt < WINDOW)
    wmask = jnp.concatenate([wmask] * Q_PER_KV, axis=0)
    _softmax_tile_update(s3, wmask, vw_ref[0, 0, pl.ds(w0, span), :], all_rows, m_scr, l_scr, acc_scr)
    o_w = finish()

    gate = jax.nn.sigmoid(gate_ref[0, 0])
    outs = []
    for r in range(Q_PER_KV):
        sl = slice(r * Q_BLOCK, (r + 1) * Q_BLOCK)
        outs.append(gate[:, 3 * r:3 * r + 1] * o_c[sl] + gate[:, 3 * r + 1:3 * r + 2] * o_s[sl]
                    + gate[:, 3 * r + 2:3 * r + 3] * o_w[sl])
    o_ref[0] = jnp.concatenate(outs, axis=1).astype(o_ref.dtype)


def nsa_prompt(q_r, q_n, zg, kc, vc, slc_kv, win_kv):
    b, t = q_r.shape[:2]
    assert t % NSA_KT == 0 and t >= WINDOW + Q_BLOCK and WINDOW % Q_BLOCK == 0
    nb = kc.shape[1]
    ncp = -(-nb // 128) * 128
    ns = t // SLC_BLOCK
    heads_first = lambda a: a.transpose(0, 2, 3, 1, 4)
    kv_first = lambda a: a.transpose(0, 2, 1, 3).astype(MXU_DTYPE)
    pad_c = lambda a: jnp.pad(a, ((0, 0), (0, ncp - nb), (0, 0), (0, 0)))
    gate_in = zg.reshape(b, t, KV_HEADS, 3 * Q_PER_KV).transpose(0, 2, 1, 3)
    assert ns % 8 == 0
    ns_pad = -(-ns // 128) * 128
    ovl = np.zeros((ncp, ns_pad), np.float32)
    ovl[:nb, :ns] = np.asarray(overlap_matrix_np(nb, ns))
    key_blk = (np.arange(t) // SLC_BLOCK).reshape(t // NSA_KT, 1, NSA_KT)
    eexp = (key_blk == np.arange(ns_pad).reshape(1, ns_pad, 1)).astype(np.float32)
    rows = Q_PER_KV * Q_BLOCK
    q_spec = pl.BlockSpec((1, 1, Q_PER_KV, Q_BLOCK, HEAD_DIM), lambda bi, g, qi: (bi, g, 0, qi, 0))
    c_spec = pl.BlockSpec((1, 1, ncp, HEAD_DIM), lambda bi, g, qi: (bi, g, 0, 0))
    kv_spec = pl.BlockSpec((1, 1, t, HEAD_DIM), lambda bi, g, qi: (bi, g, 0, 0))
    return pl.pallas_call(
        functools.partial(_nsa_prompt_kernel, ns=ns),
        out_shape=jax.ShapeDtypeStruct((b, t, BRANCH_WIDTH), MXU_DTYPE),
        grid=(b, KV_HEADS, t // Q_BLOCK),
        in_specs=[q_spec, q_spec,
                  pl.BlockSpec((1, 1, Q_BLOCK, 3 * Q_PER_KV), lambda bi, g, qi: (bi, g, qi, 0)),
                  c_spec, c_spec,
                  pl.BlockSpec((ncp, ns_pad), lambda bi, g, qi: (0, 0)),
                  pl.BlockSpec((t // NSA_KT, ns_pad, NSA_KT), lambda bi, g, qi: (0, 0, 0)),
                  kv_spec, kv_spec, kv_spec, kv_spec],
        out_specs=pl.BlockSpec((1, Q_BLOCK, Q_PER_KV * HEAD_DIM), lambda bi, g, qi: (bi, qi, g)),
        scratch_shapes=[pltpu.VMEM((rows, 1), jnp.float32),
                        pltpu.VMEM((rows, 1), jnp.float32),
                        pltpu.VMEM((rows, HEAD_DIM), jnp.float32)],
        compiler_params=pltpu.CompilerParams(
            dimension_semantics=("parallel", "parallel", "arbitrary"),
            vmem_limit_bytes=VMEM_LIMIT_BYTES),
        name="nsa_prompt",
    )(heads_first(q_n), heads_first(q_r), gate_in,
      kv_first(pad_c(kc)), kv_first(pad_c(vc)),
      jnp.asarray(ovl, MXU_DTYPE), jnp.asarray(eexp, MXU_DTYPE),
      kv_first(slc_kv[:, :, 0]), kv_first(slc_kv[:, :, 1]),
      kv_first(win_kv[:, :, 0]), kv_first(win_kv[:, :, 1]))


SEGS_PER_PAGE = PAGE_SIZE // CMP_STRIDE
SEG_WIDTH = CMP_STRIDE * HEAD_DIM
GD = KV_HEADS * HEAD_DIM


def _masked_softmax_parts(s, mask, s_new, new_on):
    m = jnp.max(jnp.where(mask, s, NEG_BIG), axis=-1, keepdims=True)
    m = jnp.maximum(m, jnp.where(new_on, s_new, NEG_BIG))
    e = jnp.where(mask, jnp.exp(s - m), 0.0)
    e_new = jnp.where(new_on, jnp.exp(s_new - m), 0.0)
    den = jnp.maximum(jnp.sum(e, axis=-1, keepdims=True) + e_new, 1e-30)
    return e, e_new, den


def _nsa_sample_kernel(pt_ref, qn_ref, qr_ref, gate_ref, new_ref, w1_ref, b0_ref, w2_ref, kn_ref,
                       ovl_ref, eexp_ref, win_ref, *rest, n_pages, past_len, win_len):
    cmp_refs = rest[:n_pages]
    slc_refs = rest[n_pages:2 * n_pages]
    o_ref = rest[2 * n_pages]
    nh = N_HEADS
    n_seg = n_pages * SEGS_PER_PAGE
    row_group = lax.broadcasted_iota(jnp.int32, (nh, 1), 0) // Q_PER_KV

    outs = []
    for c in range(2):
        x = jnp.concatenate(
            [jnp.concatenate([cmp_refs[p][0, 0, c, g] for p in range(n_pages)], axis=0)
             for g in range(KV_HEADS)], axis=0).astype(MXU_DTYPE)
        ab = jnp.dot(x, w1_ref[c], preferred_element_type=jnp.float32)
        a = ab[:, :HEAD_DIM]
        b_next = pltpu.roll(ab[:, HEAD_DIM:], KV_HEADS * n_seg - 1, axis=0)
        hid = jax.nn.gelu(a + b_next + b0_ref[c])
        outs.append(jnp.dot(hid.astype(MXU_DTYPE), w2_ref[c].astype(MXU_DTYPE),
                            preferred_element_type=jnp.float32))
    kc_all = outs[0]
    kc_all = kc_all * lax.rsqrt(jnp.mean(kc_all * kc_all, axis=-1, keepdims=True) + NORM_EPS) * kn_ref[...]
    vc_all = outs[1]

    qn = qn_ref[0].astype(MXU_DTYPE)
    s_c = jnp.zeros((nh, n_seg), jnp.float32)
    for g in range(KV_HEADS):
        kc_g = kc_all[g * n_seg:(g + 1) * n_seg].astype(MXU_DTYPE)
        s_g = lax.dot_general(qn, kc_g, _NT_DIMS, preferred_element_type=jnp.float32)
        s_c = jnp.where(row_group == g, s_g, s_c)
    blk = lax.broadcasted_iota(jnp.int32, (nh, n_seg), 1)
    cmask = (blk * CMP_STRIDE + (CMP_BLOCK - 1) <= past_len) & (blk < n_seg - CMP_RATIO + 1)
    m = jnp.max(jnp.where(cmask, s_c, NEG_BIG), axis=-1, keepdims=True)
    e = jnp.where(cmask, jnp.exp(s_c - m), 0.0)
    p = e / jnp.maximum(jnp.sum(e, axis=-1, keepdims=True), 1e-30)
    o_c = jnp.zeros((nh, HEAD_DIM), jnp.float32)
    p_bf = p.astype(MXU_DTYPE)
    psum = jnp.zeros((nh, n_seg), jnp.float32)
    for g in range(KV_HEADS):
        vc_g = vc_all[g * n_seg:(g + 1) * n_seg].astype(MXU_DTYPE)
        in_g = row_group == g
        o_c = jnp.where(in_g, jnp.dot(p_bf, vc_g, preferred_element_type=jnp.float32), o_c)
        psum = jnp.where(in_g, jnp.sum(jnp.where(in_g, p, 0.0), axis=0, keepdims=True), psum)

    p_hi = psum.astype(MXU_DTYPE)
    p_lo = (psum - p_hi.astype(jnp.float32)).astype(MXU_DTYPE)
    ovl = ovl_ref[...]
    imp = (jnp.dot(p_hi, ovl, preferred_element_type=jnp.float32)
           + jnp.dot(p_lo, ovl, preferred_element_type=jnp.float32))
    ns_pad = imp.shape[1]
    ns = past_len // SLC_BLOCK + 1
    cur = past_len // SLC_BLOCK
    j = lax.broadcasted_iota(jnp.int32, (nh, ns_pad), 1)
    forced = (j == 0) | (j == cur) | (j == cur - 1)
    adj = jnp.where(j <= cur, jnp.where(forced, FORCE_SCORE, imp), -FORCE_SCORE)
    adj = jnp.where(j < ns, adj, PEER_NEG)
    rank = jnp.zeros((nh, ns_pad), jnp.int32)
    for k in range(ns):
        col = adj[:, k:k + 1]
        beats = (col > adj) | ((col == adj) & (j > k))
        rank = rank + beats.astype(jnp.int32)
    sel = (rank < N_SELECT) & (j < ns)
    tok_mask = jnp.dot(sel.astype(MXU_DTYPE), eexp_ref[...], preferred_element_type=jnp.float32) > 0.5
    new_sel = jnp.sum(jnp.where(sel & (j == cur), 1.0, 0.0), axis=-1, keepdims=True) > 0.5

    qr = qr_ref[0]
    qr_bf = qr.astype(MXU_DTYPE)
    new = new_ref[0]
    to_mxu = lambda a: a.astype(MXU_DTYPE).astype(jnp.float32)
    s_new = jnp.sum(to_mxu(qr) * to_mxu(new[0:1]), axis=-1, keepdims=True)
    s_sel = jnp.concatenate(
        [jnp.dot(qr_bf, slc_refs[pg][0, 0, 0:GD, :].astype(MXU_DTYPE), preferred_element_type=jnp.float32)
         for pg in range(n_pages)], axis=1)
    e, e_new, den = _masked_softmax_parts(s_sel, tok_mask, s_new, new_sel)
    e_bf = e.astype(MXU_DTYPE)
    o_s = to_mxu(e_new) * to_mxu(new[1:2])
    for pg in range(n_pages):
        o_s = o_s + lax.dot_general(e_bf[:, pg * PAGE_SIZE:(pg + 1) * PAGE_SIZE],
                                    slc_refs[pg][0, 0, GD:2 * GD, :].astype(MXU_DTYPE), _NT_DIMS,
                                    preferred_element_type=jnp.float32)
    o_s = o_s / den

    sw_new = jnp.sum(to_mxu(qr) * to_mxu(new[2:3]), axis=-1, keepdims=True)
    s_w = jnp.dot(qr_bf, win_ref[0, 0, 0:GD, :].astype(MXU_DTYPE), preferred_element_type=jnp.float32)
    dlt = win_len - lax.broadcasted_iota(jnp.int32, (nh, win_len), 1)
    wmask = (dlt >= 0) & (dlt < WINDOW) & (past_len - dlt >= 0)
    e, e_new, den = _masked_softmax_parts(s_w, wmask, sw_new, jnp.full((nh, 1), True))
    o_w = (to_mxu(e_new) * to_mxu(new[3:4])
           + lax.dot_general(e.astype(MXU_DTYPE), win_ref[0, 0, GD:2 * GD, :].astype(MXU_DTYPE), _NT_DIMS,
                             preferred_element_type=jnp.float32)) / den

    def own_block(o_bd):
        out = jnp.zeros((nh, HEAD_DIM), jnp.float32)
        for g in range(KV_HEADS):
            out = jnp.where(row_group == g, o_bd[:, g * HEAD_DIM:(g + 1) * HEAD_DIM], out)
        return out

    gate = jax.nn.sigmoid(gate_ref[0])
    o_ref[0] = gate[:, 0:1] * o_c + gate[:, 1:2] * own_block(o_s) + gate[:, 2:3] * own_block(o_w)


def nsa_sample(q_r, q_n, zg, skv_new, wkv_new, cmp_pages, slc_pages, win_t, page_table, layer,
               cmp_pe, cmp_w1, cmp_b1, cmp_w2, kn):
    bs = q_r.shape[0]
    n_pages = page_table.shape[1]
    past_len = n_pages * PAGE_SIZE
    win_len = win_t.shape[-1]
    n_seg = n_pages * SEGS_PER_PAGE
    nb = n_seg - CMP_RATIO + 1
    ns = past_len // SLC_BLOCK + 1
    ns_pad = -(-ns // 128) * 128
    scale = HEAD_DIM ** -0.5
    qn = q_n.reshape(bs, N_HEADS, HEAD_DIM) * scale
    eye = jnp.eye(KV_HEADS, dtype=q_r.dtype)
    qr_bd = (q_r.reshape(bs, KV_HEADS, Q_PER_KV, 1, HEAD_DIM) * scale
             * eye[None, :, None, :, None]).reshape(bs, N_HEADS, GD)
    gate_in = zg.reshape(bs, N_HEADS, 3)
    new = jnp.concatenate([skv_new.reshape(bs, 2, GD), wkv_new.reshape(bs, 2, GD)], axis=1)
    w1r = cmp_w1.reshape(2, CMP_RATIO, SEG_WIDTH, HEAD_DIM)
    w1c = jnp.concatenate([w1r[:, r] for r in range(CMP_RATIO)], axis=-1).astype(MXU_DTYPE)
    b0 = (jnp.einsum('cld,cldh->ch', cmp_pe, cmp_w1, precision=lax.Precision.HIGHEST) + cmp_b1)[:, None, :]
    ovl = np.zeros((n_seg, ns_pad), np.float32)
    ovl[:nb, :ns] = overlap_matrix_np(nb, ns)
    eexp = (np.arange(past_len)[None, :] // SLC_BLOCK == np.arange(ns_pad)[:, None]).astype(np.float32)
    const = lambda shape: pl.BlockSpec(shape, lambda b, pt: (0,) * len(shape))
    per_b = lambda shape: pl.BlockSpec((1,) + shape, lambda b, pt: (b,) + (0,) * len(shape))
    cmp_spec = [pl.BlockSpec((1, 1, 2, KV_HEADS, SEGS_PER_PAGE, SEG_WIDTH),
                             functools.partial(lambda b, pt, pg: (layer, pt[b, pg], 0, 0, 0, 0), pg=pg))
                for pg in range(n_pages)]
    slc_spec = [pl.BlockSpec((1, 1, 2 * GD, PAGE_SIZE),
                             functools.partial(lambda b, pt, pg: (layer, pt[b, pg], 0, 0), pg=pg))
                for pg in range(n_pages)]
    grid_spec = pltpu.PrefetchScalarGridSpec(
        num_scalar_prefetch=1,
        grid=(bs,),
        in_specs=[per_b((N_HEADS, HEAD_DIM)), per_b((N_HEADS, GD)), per_b((N_HEADS, 3)), per_b((4, GD)),
                  const((2, SEG_WIDTH, CMP_RATIO * HEAD_DIM)), const((2, 1, HEAD_DIM)),
                  const((2, HEAD_DIM, HEAD_DIM)), const((1, HEAD_DIM)),
                  const((n_seg, ns_pad)), const((ns_pad, past_len)),
                  pl.BlockSpec((1, 1, 2 * GD, win_len), lambda b, pt: (layer, b, 0, 0))]
                 + cmp_spec + slc_spec,
        out_specs=per_b((N_HEADS, HEAD_DIM)),
    )
    out = pl.pallas_call(
        functools.partial(_nsa_sample_kernel, n_pages=n_pages, past_len=past_len, win_len=win_len),
        out_shape=jax.ShapeDtypeStruct((bs, N_HEADS, HEAD_DIM), jnp.float32),
        grid_spec=grid_spec,
        compiler_params=pltpu.CompilerParams(
            dimension_semantics=("arbitrary",),
            vmem_limit_bytes=VMEM_LIMIT_BYTES),
        name="nsa_sample",
    )(page_table, qn, qr_bd, gate_in, new, w1c, b0, cmp_w2, kn.reshape(1, HEAD_DIM),
      jnp.asarray(ovl, MXU_DTYPE), jnp.asarray(eexp, MXU_DTYPE), win_t,
      *([cmp_pages] * n_pages), *([slc_pages] * n_pages))
    return out.reshape(bs, 1, BRANCH_WIDTH)


def gmlp_mix(u, v, w_s, b_s):
    b, t, _ = v.shape
    nc = -(-t // CHUNK)
    vp = jnp.pad(v, ((0, 0), (0, nc * CHUNK - t), (0, 0))).reshape(b, nc, CHUNK, GMLP_GROUPS, GMLP_WIDTH // GMLP_GROUPS)
    w = w_s * jnp.tril(jnp.ones((CHUNK, CHUNK), w_s.dtype))
    mixed = jnp.einsum('gpq,bcqgd->bcpgd', w, vp) + b_s.T[None, None, :, :, None]
    return u * mixed.reshape(b, nc * CHUNK, GMLP_WIDTH)[:, :t]


def conformer_conv(zglu, buf, w_dw, b_dw, norm_g):
    a, gt = jnp.split(zglu, 2, axis=-1)
    glu = a * jax.nn.sigmoid(gt)
    xin = jnp.concatenate([buf.astype(glu.dtype), glu], axis=1)
    y = lax.conv_general_dilated(xin, w_dw[:, None, :].astype(glu.dtype), (1,), 'VALID',
                                 dimension_numbers=('NWC', 'WIO', 'NWC'), feature_group_count=CONV_WIDTH) + b_dw
    y = jax.nn.silu(rms_norm(y, norm_g))
    return y, xin[:, xin.shape[1] - (CONV_K - 1):]


MERGE_TN = 512


def _merge_kernel(h_ref, o0_ref, o1_ref, o2_ref, wm0_ref, wm1_ref, wm2_ref, bm0_ref, bm1_ref, bm2_ref,
                  wb_ref, y_ref):
    h = h_ref[...].astype(MXU_DTYPE)
    acc = None
    for i, (o_ref, wm_ref, bm_ref) in enumerate(((o0_ref, wm0_ref, bm0_ref), (o1_ref, wm1_ref, bm1_ref),
                                                 (o2_ref, wm2_ref, bm2_ref))):
        mg = jax.nn.sigmoid(jnp.dot(h, wm_ref[...].astype(MXU_DTYPE), preferred_element_type=jnp.float32)
                            + bm_ref[...])
        term = mg * jnp.dot(o_ref[...].astype(MXU_DTYPE), wb_ref[i].astype(MXU_DTYPE),
                            preferred_element_type=jnp.float32)
        acc = term if acc is None else acc + term
    y_ref[...] = acc.astype(y_ref.dtype)


def merge_branches(h, o_nsa, o_gmlp, o_conv, w_branch, w_merge, b_merge):
    m, d = h.shape
    tm = 512 if m % 512 == 0 else m
    tn = MERGE_TN
    nj = d // tn
    bw = o_nsa.shape[1]
    b2 = b_merge.reshape(1, 3 * d)
    o_spec = pl.BlockSpec((tm, bw), lambda i, j: (i, 0))
    wm_specs = [pl.BlockSpec((d, tn), functools.partial(lambda i, j, k: (0, k * nj + j), k=k)) for k in range(3)]
    bm_specs = [pl.BlockSpec((1, tn), functools.partial(lambda i, j, k: (0, k * nj + j), k=k)) for k in range(3)]
    return pl.pallas_call(
        _merge_kernel,
        out_shape=jax.ShapeDtypeStruct((m, d), MXU_DTYPE),
        grid=(m // tm, nj),
        in_specs=[pl.BlockSpec((tm, d), lambda i, j: (i, 0)), o_spec, o_spec, o_spec]
                 + wm_specs + bm_specs + [pl.BlockSpec((3, bw, tn), lambda i, j: (0, 0, j))],
        out_specs=pl.BlockSpec((tm, tn), lambda i, j: (i, j)),
        compiler_params=pltpu.CompilerParams(
            dimension_semantics=("parallel", "arbitrary"),
            vmem_limit_bytes=VMEM_LIMIT_BYTES),
        name="merge_branches",
    )(h, o_nsa, o_gmlp, o_conv, w_merge, w_merge, w_merge, b2, b2, b2, w_branch)


def mixer_merge(x, h, o_nsa, o_gmlp, o_conv, gate, w_branch, w_merge, b_merge, w_out):
    b, t, d = x.shape
    flat = lambda a: a.reshape(b * t, a.shape[-1]).astype(MXU_DTYPE)
    y = merge_branches(flat(h), flat(o_nsa), flat(o_gmlp), flat(o_conv), w_branch, w_merge, b_merge)
    return x + gate[:, None, :] * mm(y, w_out).reshape(b, t, d)


PEER_TB = 512
PEER_I1 = 8
PEER_JSUB = 4
PEER_KSUB = 32
PEER_NEG = -3.0e38
PEER_NRANK = PEER_TOPK + 1
PEER_VROWS = 24
PEER_NCAND = PEER_VROWS + 7 * 8 + 16


def _extract_top(cur_ref, out_ref, n_out):
    n = cur_ref.shape[0]
    iota = lax.broadcasted_iota(jnp.int32, cur_ref.shape, 0)
    for k in range(n_out):
        cur = cur_ref[...]
        m = jnp.max(cur, axis=0, keepdims=True)
        first = jnp.min(jnp.where(cur == m, iota, n), axis=0, keepdims=True)
        cur_ref[...] = jnp.where(iota == first, PEER_NEG, cur)
        out_ref[k:k + 1, :] = m


def _peer_route_kernel(x_ref, w_ref, k_ref, thr1_ref, e1_ref, s2_ref, e2_ref,
                       work_ref, v1_ref, v2_ref, cand_ref, ctop_ref):
    x = x_ref[...].astype(MXU_DTYPE)
    w = w_ref[...].astype(MXU_DTYPE)
    q = jnp.dot(x, w, preferred_element_type=jnp.float32)
    half = PEER_QDIM // 2
    nt = (((1,), (1,)), ((), ()))
    s = []
    for c in range(2):
        qc = q[:, c * half:(c + 1) * half].astype(MXU_DTYPE)
        kc = k_ref[0, c].astype(MXU_DTYPE)
        s.append(lax.dot_general(kc, qc, nt, preferred_element_type=jnp.float32))
    s1, s2 = s
    v1_ref[...] = jnp.full(v1_ref.shape, PEER_NEG, jnp.float32)
    v2_ref[...] = jnp.full(v2_ref.shape, PEER_NEG, jnp.float32)
    work_ref[...] = s1
    _extract_top(work_ref, v1_ref, PEER_NRANK)
    work_ref[...] = s2
    _extract_top(work_ref, v2_ref, PEER_NRANK)
    v1 = v1_ref[...]
    v2 = v2_ref[...]
    nv = PEER_VROWS
    cand_ref[0:nv, :] = v1[0:1] + v2
    for a in range(1, 8):
        cand_ref[nv + 8 * (a - 1):nv + 8 * a, :] = v1[a:a + 1] + v2[0:8]
    cand_ref[nv + 56:nv + 72, :] = v1[8:nv] + v2[0:1]
    _extract_top(cand_ref, ctop_ref, PEER_NRANK)
    tau = 0.5 * (ctop_ref[PEER_TOPK - 1:PEER_TOPK, :] + ctop_ref[PEER_TOPK:PEER_TOPK + 1, :])
    m1 = v1[0:1]
    m2 = v2[0:1]
    e1top = jnp.exp(v1 - m1)
    e2top = jnp.exp(v2 - m2)
    z = jnp.zeros_like(tau)
    for a in range(PEER_NRANK):
        sel = v2 >= (tau - v1[a:a + 1])
        z = z + jnp.sum(jnp.where(sel, e2top, 0.0), axis=0, keepdims=True) * e1top[a:a + 1]
    inv_z = 1.0 / z
    thr1_ref[0] = tau - s1
    e1_ref[0] = jnp.exp(s1 - m1) * inv_z
    s2_ref[0] = s2
    e2_ref[0] = jnp.exp(s2 - m2)


def peer_route(xm, w_pq, sub_keys):
    t, d = xm.shape
    tb = PEER_TB
    out = jax.ShapeDtypeStruct((PEER_HEADS, PEER_KEYS, t), jnp.float32)
    ospec = pl.BlockSpec((1, PEER_KEYS, tb), lambda i, h: (h, 0, i))
    return pl.pallas_call(
        _peer_route_kernel,
        out_shape=(out, out, out, out),
        grid=(t // tb, PEER_HEADS),
        in_specs=[pl.BlockSpec((tb, d), lambda i, h: (i, 0)),
                  pl.BlockSpec((d, PEER_QDIM), lambda i, h: (0, h)),
                  pl.BlockSpec((1, 2, PEER_KEYS, PEER_QDIM // 2), lambda i, h: (h, 0, 0, 0))],
        out_specs=(ospec, ospec, ospec, ospec),
        scratch_shapes=[pltpu.VMEM((PEER_KEYS, tb), jnp.float32),
                        pltpu.VMEM((PEER_VROWS, tb), jnp.float32),
                        pltpu.VMEM((PEER_VROWS, tb), jnp.float32),
                        pltpu.VMEM((PEER_NCAND, tb), jnp.float32),
                        pltpu.VMEM((PEER_VROWS, tb), jnp.float32)],
        compiler_params=pltpu.CompilerParams(
            dimension_semantics=("parallel", "arbitrary"),
            vmem_limit_bytes=VMEM_LIMIT_BYTES),
        name="peer_route",
    )(xm, w_pq, sub_keys)


def _peer_dense_kernel(xt_ref, u_ref, vt_ref, thr1_ref, e1_ref, s2_ref, e2_ref, o_ref, s_cur, w_new):
    @pl.when(pl.program_id(1) == 0)
    def _():
        o_ref[...] = jnp.zeros_like(o_ref)

    tb = o_ref.shape[1]
    nk = PEER_KEYS
    ks = PEER_KSUB
    s_cur[...] = jnp.dot(u_ref[...], xt_ref[...], preferred_element_type=jnp.float32)
    for t0 in range(0, tb, 128):
        lanes = slice(t0, t0 + 128)
        for k0 in range(0, nk, ks):
            for j0 in range(0, PEER_I1, PEER_JSUB):
                g = [jnp.zeros((ks, 128), jnp.float32) for _ in range(PEER_JSUB)]
                for h in range(PEER_HEADS):
                    s2 = s2_ref[h, k0:k0 + ks, lanes]
                    e2 = e2_ref[h, k0:k0 + ks, lanes]
                    for jj in range(PEER_JSUB):
                        j = j0 + jj
                        sel = s2 >= thr1_ref[h, j:j + 1, lanes]
                        g[jj] = g[jj] + jnp.where(sel, e2, 0.0) * e1_ref[h, j:j + 1, lanes]
                for jj in range(PEER_JSUB):
                    r0 = (j0 + jj) * nk + k0
                    act = jax.nn.gelu(s_cur[r0:r0 + ks, lanes])
                    w_new[r0:r0 + ks, lanes] = (act * g[jj]).astype(MXU_DTYPE)
    o_ref[...] += jnp.dot(vt_ref[...], w_new[...], preferred_element_type=jnp.float32)


def peer_dense(xt, u_bf, vt_bf, thr1, e1, s2, e2):
    d, t = xt.shape
    n_exp = u_bf.shape[0]
    tb = PEER_TB
    ec = PEER_I1 * PEER_KEYS
    row_spec = pl.BlockSpec((PEER_HEADS, PEER_I1, tb), lambda i, c: (0, c, i))
    full_spec = pl.BlockSpec((PEER_HEADS, PEER_KEYS, tb), lambda i, c: (0, 0, i))
    return pl.pallas_call(
        _peer_dense_kernel,
        out_shape=jax.ShapeDtypeStruct((d, t), jnp.float32),
        grid=(t // tb, n_exp // ec),
        in_specs=[pl.BlockSpec((d, tb), lambda i, c: (0, i)),
                  pl.BlockSpec((ec, d), lambda i, c: (c, 0)),
                  pl.BlockSpec((d, ec), lambda i, c: (0, c)),
                  row_spec, row_spec, full_spec, full_spec],
        out_specs=pl.BlockSpec((d, tb), lambda i, c: (0, i)),
        scratch_shapes=[pltpu.VMEM((ec, tb), jnp.float32), pltpu.VMEM((ec, tb), MXU_DTYPE)],
        compiler_params=pltpu.CompilerParams(
            dimension_semantics=("parallel", "arbitrary"),
            vmem_limit_bytes=VMEM_LIMIT_BYTES),
        name="peer_dense",
    )(xt, u_bf, vt_bf, thr1, e1, s2, e2)


def peer_ffn(xm, w_pq, sub_keys, u_bf, vt_bf):
    t = sum(part.shape[0] for part in xm)
    d = xm[0].shape[1]
    tp = -(-t // PEER_TB) * PEER_TB
    xm_p = jnp.concatenate(list(xm) + [jnp.zeros((tp - t, d), xm[0].dtype)], axis=0)
    thr1, e1, s2, e2 = peer_route(xm_p, w_pq, sub_keys)
    out_t = peer_dense(xm_p.T.astype(MXU_DTYPE), u_bf, vt_bf, thr1, e1, s2, e2)
    return out_t.T[:t]


def kernel(x_prompt, x_sample, cache_cmp_kv, cache_slc_kv, state_win_kv, state_conv, page_table,
           c_prompt, c_sample, w_ada, b_ada, norm_mix, norm_ffn, w_in, q_norm, k_norm,
           cmp_pe, cmp_w1, cmp_b1, cmp_w2, gmlp_norm, gmlp_ws, gmlp_bs, conv_w, conv_b, conv_norm,
           w_branch, w_merge, b_merge, w_out, peer_wq, peer_keys, peer_u, peer_v):
    bp, t = x_prompt.shape[:2]
    bs, ds = x_sample.shape[:2]
    depth = w_in.shape[0]
    past_len = page_table.shape[1] * PAGE_SIZE
    pos_p = jnp.arange(t, dtype=jnp.int32)
    pos_s = past_len + jnp.arange(ds, dtype=jnp.int32)
    xp, xs = x_prompt, x_sample
    assert ds == 1
    n_phys = cache_cmp_kv.shape[1]
    cmp_pages = cache_cmp_kv.reshape(depth, n_phys, SEGS_PER_PAGE, CMP_STRIDE, 2, KV_HEADS, HEAD_DIM)
    cmp_pages = cmp_pages.transpose(0, 1, 4, 5, 2, 3, 6).reshape(
        depth, n_phys, 2, KV_HEADS, SEGS_PER_PAGE, SEG_WIDTH)
    slc_pages = cache_slc_kv.transpose(0, 1, 3, 4, 5, 2).reshape(depth, n_phys, 2 * GD, PAGE_SIZE)
    win_t = state_win_kv.transpose(0, 1, 3, 4, 5, 2).reshape(depth, bs, 2 * GD, state_win_kv.shape[2])
    cmp_p, cmp_s, slc_p, slc_s, win_p, win_s, conv_p, conv_s, gv_s = [], [], [], [], [], [], [], [], []
    for l in range(depth):
        mp, msm = adaln(c_prompt, c_sample, w_ada[l], b_ada[l])
        w_in_l, w_merge_l, w_branch_l, w_out_l = (w.astype(MXU_DTYPE) for w in
                                                  (w_in[l], w_merge[l], w_branch[l], w_out[l]))
        hp, q_r, q_n, gates, zg, ckv, skv, wkv, u, v, zglu = mixer_front(
            xp, mp[0], mp[1], norm_mix[l], w_in_l, q_norm[l], k_norm[l], gmlp_norm[l], pos_p)
        kc, vc, c_end = compress(ckv, cmp_pe[l], cmp_w1[l], cmp_b1[l], cmp_w2[l], k_norm[l, 0])
        o_nsa = nsa_prompt(q_r, q_n, zg, kc, vc, skv, wkv)
        o_g = gmlp_mix(u, v, gmlp_ws[l], gmlp_bs[l])
        o_c, buf = conformer_conv(zglu, jnp.zeros((bp, CONV_K - 1, CONV_WIDTH), zglu.dtype),
                                  conv_w[l], conv_b[l], conv_norm[l])
        xp = mixer_merge(xp, hp, o_nsa, o_g, o_c, mp[2], w_branch_l, w_merge_l, b_merge[l], w_out_l)
        cmp_p.append(ckv)
        slc_p.append(skv)
        win_p.append(wkv[:, t - min(WINDOW, t):])
        conv_p.append(buf)
        hs, q_r, q_n, gates, zg, ckv, skv, wkv, u, v, zglu = mixer_front(
            xs, msm[0], msm[1], norm_mix[l], w_in_l, q_norm[l], k_norm[l], gmlp_norm[l], pos_s)
        win_full = jnp.concatenate([state_win_kv[l], wkv], axis=1)
        o_nsa = nsa_sample(q_r, q_n, zg, skv, wkv, cmp_pages, slc_pages, win_t, page_table, l,
                           cmp_pe[l], cmp_w1[l], cmp_b1[l], cmp_w2[l], k_norm[l, 0])
        o_g = gmlp_mix(u, v, gmlp_ws[l], gmlp_bs[l])
        o_c, buf = conformer_conv(zglu, state_conv[l], conv_w[l], conv_b[l], conv_norm[l])
        xs = mixer_merge(xs, hs, o_nsa, o_g, o_c, msm[2], w_branch_l, w_merge_l, b_merge[l], w_out_l)
        d = xp.shape[-1]
        xm_all = (modulate(xp, norm_ffn[l], mp[3], mp[4]).reshape(bp * t, d),
                  modulate(xs, norm_ffn[l], msm[3], msm[4]).reshape(bs * ds, d))
        ffn = peer_ffn(xm_all, peer_wq[l].astype(MXU_DTYPE), peer_keys[l],
                       peer_u[l].astype(MXU_DTYPE), peer_v[l].T.astype(MXU_DTYPE))
        xp = xp + mp[5][:, None, :] * ffn[:bp * t].reshape(bp, t, d)
        xs = xs + msm[5][:, None, :] * ffn[bp * t:].reshape(bs, ds, d)
        cmp_s.append(ckv)
        slc_s.append(skv)
        lw = win_full.shape[1]
        win_s.append(win_full[:, lw - min(WINDOW, lw):])
        conv_s.append(buf)
        gv_s.append(v)
    return (xp, xs, jnp.stack(cmp_p), jnp.stack(cmp_s), jnp.stack(slc_p), jnp.stack(slc_s),
            jnp.stack(win_p), jnp.stack(win_s), jnp.stack(conv_p), jnp.stack(conv_s), jnp.stack(gv_s))
```

```python
import functools

import jax
import jax.numpy as jnp
import numpy as np
from jax import lax
from jax.experimental import pallas as pl
from jax.experimental.pallas import tpu as pltpu

D_MODEL = 2048
PAGE_SIZE = 128
BRANCH_WIDTH = D_MODEL // 2
HEAD_DIM = 64
N_HEADS = BRANCH_WIDTH // HEAD_DIM
KV_HEADS = 4
Q_PER_KV = N_HEADS // KV_HEADS
KV_WIDTH = KV_HEADS * HEAD_DIM
ROT_DIM = HEAD_DIM // 4
ROPE_THETA = 500000.0
CMP_BLOCK = 32
CMP_STRIDE = 16
CMP_RATIO = CMP_BLOCK // CMP_STRIDE
SLC_BLOCK = 64
N_SELECT = 16
WINDOW = 512
Q_BLOCK = 128
GMLP_WIDTH = BRANCH_WIDTH
GMLP_GROUPS = 8
CHUNK = 128
CONV_WIDTH = BRANCH_WIDTH
CONV_K = 31
PEER_HEADS = 8
PEER_KEYS = 128
PEER_QDIM = 256
PEER_TOPK = 16
PEER_BLOCK = 128
NORM_EPS = 1e-6
NEG_BIG = -1e30
FORCE_SCORE = 1e9

VMEM_LIMIT_BYTES = 56 * 1024 * 1024
MXU_DTYPE = jnp.bfloat16


def _mm_kernel(a_ref, b_ref, o_ref):
    o_ref[...] = jnp.dot(a_ref[...].astype(MXU_DTYPE), b_ref[...].astype(MXU_DTYPE),
                         preferred_element_type=jnp.float32)


def _pick_tm(m):
    for tm in (1024, 512, 256, 128):
        if m % tm == 0:
            return tm
    return m


def mm(a, b, tn=1024):
    m, k = a.shape
    _, n = b.shape
    tm = _pick_tm(m)
    tn = min(tn, n)
    if m > tm:
        a = a.astype(MXU_DTYPE)
    if jnp.dtype(b.dtype).itemsize > 2:
        tn = min(tn, 512)
    return pl.pallas_call(
        _mm_kernel,
        out_shape=jax.ShapeDtypeStruct((m, n), jnp.float32),
        grid=(m // tm, pl.cdiv(n, tn)),
        in_specs=[pl.BlockSpec((tm, k), lambda i, j: (i, 0)),
                  pl.BlockSpec((k, tn), lambda i, j: (0, j))],
        out_specs=pl.BlockSpec((tm, tn), lambda i, j: (i, j)),
        compiler_params=pltpu.CompilerParams(
            dimension_semantics=("parallel", "arbitrary"),
            vmem_limit_bytes=VMEM_LIMIT_BYTES),
        name="mm",
    )(a, b)


def mm3(x, w):
    b, t, k = x.shape
    return mm(x.reshape(b * t, k), w).reshape(b, t, w.shape[1])


def rms_norm(x, g):
    xf = x.astype(jnp.float32)
    y = xf * lax.rsqrt(jnp.mean(xf * xf, axis=-1, keepdims=True) + NORM_EPS)
    return (y * g.astype(jnp.float32)).astype(x.dtype)


def masked_softmax(s, mask):
    s = s.astype(jnp.float32)
    m = jnp.max(jnp.where(mask, s, NEG_BIG), axis=-1, keepdims=True)
    e = jnp.where(mask, jnp.exp(s - m), 0.0)
    return e / jnp.maximum(jnp.sum(e, axis=-1, keepdims=True), 1e-30)


def rope_partial(x, pos):
    half = ROT_DIM // 2
    freqs = ROPE_THETA ** (-jnp.arange(half, dtype=jnp.float32) / half)
    ang = pos.astype(jnp.float32)[:, None] * freqs[None, :]
    cos = jnp.cos(ang)[None, :, None, :]
    sin = jnp.sin(ang)[None, :, None, :]
    xr = x[..., :ROT_DIM].astype(jnp.float32)
    x1, x2 = xr[..., :half], xr[..., half:]
    rot = jnp.concatenate([x1 * cos - x2 * sin, x1 * sin + x2 * cos], axis=-1)
    return jnp.concatenate([rot.astype(x.dtype), x[..., ROT_DIM:]], axis=-1)


def adaln(c_prompt, c_sample, w_ada, b_ada):
    bp, bs = c_prompt.shape[0], c_sample.shape[0]
    rows = bp + bs
    pad = (-rows) % 8
    c_all = jnp.concatenate([c_prompt, c_sample, jnp.zeros((pad, c_prompt.shape[1]), c_prompt.dtype)], axis=0)
    mod = mm(jax.nn.silu(c_all), w_ada) + b_ada
    return jnp.split(mod[:bp], 6, axis=-1), jnp.split(mod[bp:rows], 6, axis=-1)


def modulate(x, g, shift, scale):
    return rms_norm(x, g) * (1.0 + scale[:, None, :]) + shift[:, None, :]


def mixer_front(x, shift, scale, norm_g, w_in, q_norm, k_norm, gmlp_norm, pos):
    b, t, _ = x.shape
    h = modulate(x, norm_g, shift, scale)
    z = mm3(h, w_in)
    sizes = [BRANCH_WIDTH, 6 * KV_WIDTH, 3 * N_HEADS, GMLP_WIDTH, GMLP_WIDTH, 2 * CONV_WIDTH]
    zq, zkv, zg, zu, zv, zglu = jnp.split(z, np.cumsum(sizes)[:-1].tolist(), axis=-1)
    q = rms_norm(zq.reshape(b, t, N_HEADS, HEAD_DIM), q_norm)
    q_n = q.reshape(b, t, KV_HEADS, Q_PER_KV, HEAD_DIM)
    q_r = rope_partial(q, pos).reshape(b, t, KV_HEADS, Q_PER_KV, HEAD_DIM)
    kv = zkv.reshape(b, t, 3, 2, KV_HEADS, HEAD_DIM)
    cmp_kv = kv[:, :, 0]
    ks = rope_partial(rms_norm(kv[:, :, 1, 0], k_norm[1]), pos)
    slc_kv = jnp.stack([ks, kv[:, :, 1, 1]], axis=2)
    kw = rope_partial(rms_norm(kv[:, :, 2, 0], k_norm[2]), pos)
    win_kv = jnp.stack([kw, kv[:, :, 2, 1]], axis=2)
    gates = jax.nn.sigmoid(zg.reshape(b, t, KV_HEADS, Q_PER_KV, 3))
    u = jax.nn.gelu(zu)
    v = rms_norm(jax.nn.gelu(zv), gmlp_norm)
    return h, q_r, q_n, gates, zg, cmp_kv, slc_kv, win_kv, u, v, zglu


def compress(cmp_kv, pe, w1, b1, w2, kn):
    b, L = cmp_kv.shape[:2]
    n_seg = L // CMP_STRIDE
    nb = n_seg - CMP_RATIO + 1
    seg = cmp_kv[:, :n_seg * CMP_STRIDE].reshape(b, n_seg, CMP_STRIDE, 2, KV_HEADS, HEAD_DIM)
    w1r = w1.reshape(2, CMP_RATIO, CMP_STRIDE, HEAD_DIM, HEAD_DIM)
    acc = (jnp.einsum('cld,cldh->ch', pe, w1) + b1)[:, None, :]
    for r in range(CMP_RATIO):
        acc = acc + jnp.einsum('bnlcgd,cldh->bncgh', seg[:, r:r + nb], w1r[:, r])
    out = jnp.einsum('bncgh,chd->bncgd', jax.nn.gelu(acc), w2)
    kc = rms_norm(out[:, :, 0], kn)
    vc = out[:, :, 1]
    c_end = jnp.asarray(np.arange(nb) * CMP_STRIDE + CMP_BLOCK - 1, dtype=jnp.int32)
    return kc, vc, c_end


def slc_blocks(slc_kv):
    b, L = slc_kv.shape[:2]
    ns = -(-L // SLC_BLOCK)
    kv = jnp.pad(slc_kv, ((0, 0), (0, ns * SLC_BLOCK - L), (0, 0), (0, 0), (0, 0)))
    kv = kv.reshape(b, ns, SLC_BLOCK, 2, KV_HEADS, HEAD_DIM).transpose(3, 0, 4, 1, 2, 5)
    return kv[0], kv[1]


def overlap_matrix_np(nb, ns):
    cs = np.arange(nb) * CMP_STRIDE
    ce = cs + CMP_BLOCK - 1
    ss = np.arange(ns) * SLC_BLOCK
    se = ss + SLC_BLOCK - 1
    return ((cs[:, None] <= se[None, :]) & (ce[:, None] >= ss[None, :])).astype(np.float32)


def overlap_matrix(nb, ns):
    return jnp.asarray(overlap_matrix_np(nb, ns))


def nsa_attend(q_r, q_n, gates, pos_q, kc, vc, c_end, ovl, sk, sv, wk, wv, w_pos):
    b, tq = q_r.shape[:2]
    scale = HEAD_DIM ** -0.5
    s = jnp.einsum('bqgrd,bngd->bqgrn', q_n, kc) * scale
    p = masked_softmax(s, (c_end[None, :] <= pos_q[:, None])[None, :, None, None, :])
    o_c = jnp.einsum('bqgrn,bngd->bqgrd', p.astype(vc.dtype), vc)
    imp = jnp.einsum('bqgrn,ns->bgqs', p, ovl)
    ns = ovl.shape[1]
    j = jnp.arange(ns, dtype=jnp.int32)[None, :]
    cur = (pos_q // SLC_BLOCK)[:, None]
    valid = j <= cur
    forced = (j == 0) | (j == cur) | (j == cur - 1)
    imp = jnp.where(valid[None, None], jnp.where(forced[None, None], FORCE_SCORE, imp), -FORCE_SCORE)
    n_sel = min(N_SELECT, ns)
    sel = lax.top_k(imp, n_sel)[1]
    gather = jax.vmap(jax.vmap(lambda kb, si: kb[si]))
    ksel = gather(sk, sel)
    vsel = gather(sv, sel)
    tok = sel[..., None] * SLC_BLOCK + jnp.arange(SLC_BLOCK, dtype=jnp.int32)
    ms = (tok <= pos_q[None, None, :, None, None]).transpose(0, 2, 1, 3, 4).reshape(b, tq, KV_HEADS, 1, n_sel * SLC_BLOCK)
    s2 = jnp.einsum('bqgrd,bgqnld->bqgrnl', q_r, ksel).reshape(b, tq, KV_HEADS, Q_PER_KV, n_sel * SLC_BLOCK) * scale
    p2 = masked_softmax(s2, ms)
    o_s = jnp.einsum('bqgrk,bgqkd->bqgrd', p2.astype(vsel.dtype), vsel.reshape(b, KV_HEADS, tq, n_sel * SLC_BLOCK, HEAD_DIM))
    s3 = jnp.einsum('bqgrd,bkgd->bqgrk', q_r, wk) * scale
    dlt = pos_q[:, None] - w_pos[None, :]
    mw = (dlt >= 0) & (dlt < WINDOW) & (w_pos[None, :] >= 0)
    p3 = masked_softmax(s3, mw[None, :, None, None, :])
    o_w = jnp.einsum('bqgrk,bkgd->bqgrd', p3.astype(wv.dtype), wv)
    return gates[..., 0:1] * o_c + gates[..., 1:2] * o_s + gates[..., 2:3] * o_w


NSA_KT = 1024
_NT_DIMS = (((1,), (1,)), ((), ()))


def _softmax_tile_update(s, mask, v, sl, m_scr, l_scr, acc_scr):
    m_old = m_scr[sl]
    m_new = jnp.maximum(m_old, jnp.max(jnp.where(mask, s, NEG_BIG), axis=-1, keepdims=True))
    alpha = jnp.exp(m_old - m_new)
    e = jnp.where(mask, jnp.exp(s - m_new), 0.0)
    l_scr[sl] = alpha * l_scr[sl] + jnp.sum(e, axis=-1, keepdims=True)
    acc_scr[sl] = alpha * acc_scr[sl] + jnp.dot(e.astype(MXU_DTYPE), v, preferred_element_type=jnp.float32)
    m_scr[sl] = m_new


def _nsa_prompt_kernel(qn_ref, qr_ref, gate_ref, kc_ref, vc_ref, ovl_ref, eexp_ref,
                       ks_ref, vs_ref, kw_ref, vw_ref, o_ref, m_scr, l_scr, acc_scr, *, ns):
    qi = pl.program_id(2)
    q0 = qi * Q_BLOCK
    rows = Q_PER_KV * Q_BLOCK
    scale = HEAD_DIM ** -0.5
    qn = (qn_ref[0, 0].reshape(rows, HEAD_DIM) * scale).astype(MXU_DTYPE)
    qr = (qr_ref[0, 0].reshape(rows, HEAD_DIM) * scale).astype(MXU_DTYPE)
    pos_row = q0 + (lax.broadcasted_iota(jnp.int32, (rows, 1), 0) & (Q_BLOCK - 1))

    kc = kc_ref[0, 0]
    ncp = kc.shape[0]
    s = lax.dot_general(qn, kc, _NT_DIMS, preferred_element_type=jnp.float32)
    c_end = lax.broadcasted_iota(jnp.int32, (rows, ncp), 1) * CMP_STRIDE + (CMP_BLOCK - 1)
    cmask = c_end <= pos_row
    m = jnp.max(jnp.where(cmask, s, NEG_BIG), axis=-1, keepdims=True)
    e = jnp.where(cmask, jnp.exp(s - m), 0.0)
    p = e / jnp.maximum(jnp.sum(e, axis=-1, keepdims=True), 1e-30)
    o_c = jnp.dot(p.astype(MXU_DTYPE), vc_ref[0, 0], preferred_element_type=jnp.float32)

    psum = p[0:Q_BLOCK]
    for r in range(1, Q_PER_KV):
        psum = psum + p[r * Q_BLOCK:(r + 1) * Q_BLOCK]
    p_hi = psum.astype(MXU_DTYPE)
    p_lo = (psum - p_hi.astype(jnp.float32)).astype(MXU_DTYPE)
    ovl = ovl_ref[...]
    imp = (jnp.dot(p_hi, ovl, preferred_element_type=jnp.float32)
           + jnp.dot(p_lo, ovl, preferred_element_type=jnp.float32))
    ns_pad = imp.shape[1]
    j = lax.broadcasted_iota(jnp.int32, (Q_BLOCK, ns_pad), 1)
    cur = (q0 + lax.broadcasted_iota(jnp.int32, (Q_BLOCK, 1), 0)) // SLC_BLOCK
    forced = (j == 0) | (j == cur) | (j == cur - 1)
    adj = jnp.where(j <= cur, jnp.where(forced, FORCE_SCORE, imp), -FORCE_SCORE)
    a = adj.T[0:ns]
    jrow = lax.broadcasted_iota(jnp.int32, (ns, Q_BLOCK), 0)
    rank = jnp.zeros((ns, Q_BLOCK), jnp.int32)
    for k in range(ns):
        row = a[k:k + 1, :]
        beats = (row > a) | ((row == a) & (jrow > k))
        rank = rank + beats.astype(jnp.int32)
    sel_t = (rank < N_SELECT).astype(jnp.float32)
    if ns_pad > ns:
        sel_t = jnp.concatenate([sel_t, jnp.zeros((ns_pad - ns, Q_BLOCK), jnp.float32)], axis=0)
    sel = sel_t.T.astype(MXU_DTYPE)

    def init():
        m_scr[...] = jnp.full(m_scr.shape, NEG_BIG, jnp.float32)
        l_scr[...] = jnp.zeros(l_scr.shape, jnp.float32)
        acc_scr[...] = jnp.zeros(acc_scr.shape, jnp.float32)

    def finish():
        return acc_scr[...] / jnp.maximum(l_scr[...], 1e-30)

    init()
    pos_q = pos_row[0:Q_BLOCK]
    all_rows = slice(0, rows)

    def slc_body(kt, carry):
        k0 = pl.multiple_of(kt * NSA_KT, NSA_KT)
        s2 = lax.dot_general(qr, ks_ref[0, 0, pl.ds(k0, NSA_KT), :], _NT_DIMS,
                             preferred_element_type=jnp.float32)
        bm = jnp.dot(sel, eexp_ref[kt], preferred_element_type=jnp.float32)
        kpos = k0 + lax.broadcasted_iota(jnp.int32, (Q_BLOCK, NSA_KT), 1)
        mask = (bm > 0.5) & (kpos <= pos_q)
        mask = jnp.concatenate([mask] * Q_PER_KV, axis=0)
        _softmax_tile_update(s2, mask, vs_ref[0, 0, pl.ds(k0, NSA_KT), :], all_rows, m_scr, l_scr, acc_scr)
        return carry

    lax.fori_loop(0, (q0 + Q_BLOCK + NSA_KT - 1) // NSA_KT, slc_body, 0)
    o_s = finish()

    init()
    w0 = pl.multiple_of(jnp.maximum(q0 - WINDOW, 0), Q_BLOCK)
    span = WINDOW + Q_BLOCK
    s3 = lax.dot_general(qr, kw_ref[0, 0, pl.ds(w0, span), :], _NT_DIMS,
                         preferred_element_type=jnp.float32)
    dlt = pos_q - (w0 + lax.broadcasted_iota(jnp.int32, (Q_BLOCK, span), 1))
    wmask = (dlt >= 0) & (dlt < WINDOW)
    wmask = jnp.concatenate([wmask] * Q_PER_KV, axis=0)
    _softmax_tile_update(s3, wmask, vw_ref[0, 0, pl.ds(w0, span), :], all_rows, m_scr, l_scr, acc_scr)
    o_w = finish()

    gate = jax.nn.sigmoid(gate_ref[0, 0])
    outs = []
    for r in range(Q_PER_KV):
        sl = slice(r * Q_BLOCK, (r + 1) * Q_BLOCK)
        outs.append(gate[:, 3 * r:3 * r + 1] * o_c[sl] + gate[:, 3 * r + 1:3 * r + 2] * o_s[sl]
                    + gate[:, 3 * r + 2:3 * r + 3] * o_w[sl])
    o_ref[0] = jnp.concatenate(outs, axis=1).astype(o_ref.dtype)


def nsa_prompt(q_r, q_n, zg, kc, vc, slc_kv, win_kv):
    b, t = q_r.shape[:2]
    assert t % NSA_KT == 0 and t >= WINDOW + Q_BLOCK and WINDOW % Q_BLOCK == 0
    nb = kc.shape[1]
    ncp = -(-nb // 128) * 128
    ns = t // SLC_BLOCK
    heads_first = lambda a: a.transpose(0, 2, 3, 1, 4)
    kv_first = lambda a: a.transpose(0, 2, 1, 3).astype(MXU_DTYPE)
    pad_c = lambda a: jnp.pad(a, ((0, 0), (0, ncp - nb), (0, 0), (0, 0)))
    gate_in = zg.reshape(b, t, KV_HEADS, 3 * Q_PER_KV).transpose(0, 2, 1, 3)
    assert ns % 8 == 0
    ns_pad = -(-ns // 128) * 128
    ovl = np.zeros((ncp, ns_pad), np.float32)
    ovl[:nb, :ns] = np.asarray(overlap_matrix_np(nb, ns))
    key_blk = (np.arange(t) // SLC_BLOCK).reshape(t // NSA_KT, 1, NSA_KT)
    eexp = (key_blk == np.arange(ns_pad).reshape(1, ns_pad, 1)).astype(np.float32)
    rows = Q_PER_KV * Q_BLOCK
    q_spec = pl.BlockSpec((1, 1, Q_PER_KV, Q_BLOCK, HEAD_DIM), lambda bi, g, qi: (bi, g, 0, qi, 0))
    c_spec = pl.BlockSpec((1, 1, ncp, HEAD_DIM), lambda bi, g, qi: (bi, g, 0, 0))
    kv_spec = pl.BlockSpec((1, 1, t, HEAD_DIM), lambda bi, g, qi: (bi, g, 0, 0))
    return pl.pallas_call(
        functools.partial(_nsa_prompt_kernel, ns=ns),
        out_shape=jax.ShapeDtypeStruct((b, t, BRANCH_WIDTH), MXU_DTYPE),
        grid=(b, KV_HEADS, t // Q_BLOCK),
        in_specs=[q_spec, q_spec,
                  pl.BlockSpec((1, 1, Q_BLOCK, 3 * Q_PER_KV), lambda bi, g, qi: (bi, g, qi, 0)),
                  c_spec, c_spec,
                  pl.BlockSpec((ncp, ns_pad), lambda bi, g, qi: (0, 0)),
                  pl.BlockSpec((t // NSA_KT, ns_pad, NSA_KT), lambda bi, g, qi: (0, 0, 0)),
                  kv_spec, kv_spec, kv_spec, kv_spec],
        out_specs=pl.BlockSpec((1, Q_BLOCK, Q_PER_KV * HEAD_DIM), lambda bi, g, qi: (bi, qi, g)),
        scratch_shapes=[pltpu.VMEM((rows, 1), jnp.float32),
                        pltpu.VMEM((rows, 1), jnp.float32),
                        pltpu.VMEM((rows, HEAD_DIM), jnp.float32)],
        compiler_params=pltpu.CompilerParams(
            dimension_semantics=("parallel", "parallel", "arbitrary"),
            vmem_limit_bytes=VMEM_LIMIT_BYTES),
        name="nsa_prompt",
    )(heads_first(q_n), heads_first(q_r), gate_in,
      kv_first(pad_c(kc)), kv_first(pad_c(vc)),
      jnp.asarray(ovl, MXU_DTYPE), jnp.asarray(eexp, MXU_DTYPE),
      kv_first(slc_kv[:, :, 0]), kv_first(slc_kv[:, :, 1]),
      kv_first(win_kv[:, :, 0]), kv_first(win_kv[:, :, 1]))


SEGS_PER_PAGE = PAGE_SIZE // CMP_STRIDE
SEG_WIDTH = CMP_STRIDE * HEAD_DIM
GD = KV_HEADS * HEAD_DIM


def _masked_softmax_parts(s, mask, s_new, new_on):
    m = jnp.max(jnp.where(mask, s, NEG_BIG), axis=-1, keepdims=True)
    m = jnp.maximum(m, jnp.where(new_on, s_new, NEG_BIG))
    e = jnp.where(mask, jnp.exp(s - m), 0.0)
    e_new = jnp.where(new_on, jnp.exp(s_new - m), 0.0)
    den = jnp.maximum(jnp.sum(e, axis=-1, keepdims=True) + e_new, 1e-30)
    return e, e_new, den


def _nsa_sample_kernel(pt_ref, qn_ref, qr_ref, gate_ref, new_ref, w1_ref, b0_ref, w2_ref, kn_ref,
                       ovl_ref, eexp_ref, win_ref, *rest, n_pages, past_len, win_len):
    cmp_refs = rest[:n_pages]
    slc_refs = rest[n_pages:2 * n_pages]
    o_ref = rest[2 * n_pages]
    nh = N_HEADS
    n_seg = n_pages * SEGS_PER_PAGE
    row_group = lax.broadcasted_iota(jnp.int32, (nh, 1), 0) // Q_PER_KV

    outs = []
    for c in range(2):
        x = jnp.concatenate(
            [jnp.concatenate([cmp_refs[p][0, 0, c, g] for p in range(n_pages)], axis=0)
             for g in range(KV_HEADS)], axis=0).astype(MXU_DTYPE)
        ab = jnp.dot(x, w1_ref[c], preferred_element_type=jnp.float32)
        a = ab[:, :HEAD_DIM]
        b_next = pltpu.roll(ab[:, HEAD_DIM:], KV_HEADS * n_seg - 1, axis=0)
        hid = jax.nn.gelu(a + b_next + b0_ref[c])
        outs.append(jnp.dot(hid.astype(MXU_DTYPE), w2_ref[c].astype(MXU_DTYPE),
                            preferred_element_type=jnp.float32))
    kc_all = outs[0]
    kc_all = kc_all * lax.rsqrt(jnp.mean(kc_all * kc_all, axis=-1, keepdims=True) + NORM_EPS) * kn_ref[...]
    vc_all = outs[1]

    qn = qn_ref[0].astype(MXU_DTYPE)
    s_c = jnp.zeros((nh, n_seg), jnp.float32)
    for g in range(KV_HEADS):
        kc_g = kc_all[g * n_seg:(g + 1) * n_seg].astype(MXU_DTYPE)
        s_g = lax.dot_general(qn, kc_g, _NT_DIMS, preferred_element_type=jnp.float32)
        s_c = jnp.where(row_group == g, s_g, s_c)
    blk = lax.broadcasted_iota(jnp.int32, (nh, n_seg), 1)
    cmask = (blk * CMP_STRIDE + (CMP_BLOCK - 1) <= past_len) & (blk < n_seg - CMP_RATIO + 1)
    m = jnp.max(jnp.where(cmask, s_c, NEG_BIG), axis=-1, keepdims=True)
    e = jnp.where(cmask, jnp.exp(s_c - m), 0.0)
    p = e / jnp.maximum(jnp.sum(e, axis=-1, keepdims=True), 1e-30)
    o_c = jnp.zeros((nh, HEAD_DIM), jnp.float32)
    p_bf = p.astype(MXU_DTYPE)
    psum = jnp.zeros((nh, n_seg), jnp.float32)
    for g in range(KV_HEADS):
        vc_g = vc_all[g * n_seg:(g + 1) * n_seg].astype(MXU_DTYPE)
        in_g = row_group == g
        o_c = jnp.where(in_g, jnp.dot(p_bf, vc_g, preferred_element_type=jnp.float32), o_c)
        psum = jnp.where(in_g, jnp.sum(jnp.where(in_g, p, 0.0), axis=0, keepdims=True), psum)

    p_hi = psum.astype(MXU_DTYPE)
    p_lo = (psum - p_hi.astype(jnp.float32)).astype(MXU_DTYPE)
    ovl = ovl_ref[...]
    imp = (jnp.dot(p_hi, ovl, preferred_element_type=jnp.float32)
           + jnp.dot(p_lo, ovl, preferred_element_type=jnp.float32))
    ns_pad = imp.shape[1]
    ns = past_len // SLC_BLOCK + 1
    cur = past_len // SLC_BLOCK
    j = lax.broadcasted_iota(jnp.int32, (nh, ns_pad), 1)
    forced = (j == 0) | (j == cur) | (j == cur - 1)
    adj = jnp.where(j <= cur, jnp.where(forced, FORCE_SCORE, imp), -FORCE_SCORE)
    adj = jnp.where(j < ns, adj, PEER_NEG)
    rank = jnp.zeros((nh, ns_pad), jnp.int32)
    for k in range(ns):
        col = adj[:, k:k + 1]
        beats = (col > adj) | ((col == adj) & (j > k))
        rank = rank + beats.astype(jnp.int32)
    sel = (rank < N_SELECT) & (j < ns)
    tok_mask = jnp.dot(sel.astype(MXU_DTYPE), eexp_ref[...], preferred_element_type=jnp.float32) > 0.5
    new_sel = jnp.sum(jnp.where(sel & (j == cur), 1.0, 0.0), axis=-1, keepdims=True) > 0.5

    qr = qr_ref[0]
    qr_bf = qr.astype(MXU_DTYPE)
    new = new_ref[0]
    to_mxu = lambda a: a.astype(MXU_DTYPE).astype(jnp.float32)
    s_new = jnp.sum(to_mxu(qr) * to_mxu(new[0:1]), axis=-1, keepdims=True)
    s_sel = jnp.concatenate(
        [jnp.dot(qr_bf, slc_refs[pg][0, 0, 0:GD, :].astype(MXU_DTYPE), preferred_element_type=jnp.float32)
         for pg in range(n_pages)], axis=1)
    e, e_new, den = _masked_softmax_parts(s_sel, tok_mask, s_new, new_sel)
    e_bf = e.astype(MXU_DTYPE)
    o_s = to_mxu(e_new) * to_mxu(new[1:2])
    for pg in range(n_pages):
        o_s = o_s + lax.dot_general(e_bf[:, pg * PAGE_SIZE:(pg + 1) * PAGE_SIZE],
                                    slc_refs[pg][0, 0, GD:2 * GD, :].astype(MXU_DTYPE), _NT_DIMS,
                                    preferred_element_type=jnp.float32)
    o_s = o_s / den

    sw_new = jnp.sum(to_mxu(qr) * to_mxu(new[2:3]), axis=-1, keepdims=True)
    s_w = jnp.dot(qr_bf, win_ref[0, 0, 0:GD, :].astype(MXU_DTYPE), preferred_element_type=jnp.float32)
    dlt = win_len - lax.broadcasted_iota(jnp.int32, (nh, win_len), 1)
    wmask = (dlt >= 0) & (dlt < WINDOW) & (past_len - dlt >= 0)
    e, e_new, den = _masked_softmax_parts(s_w, wmask, sw_new, jnp.full((nh, 1), True))
    o_w = (to_mxu(e_new) * to_mxu(new[3:4])
           + lax.dot_general(e.astype(MXU_DTYPE), win_ref[0, 0, GD:2 * GD, :].astype(MXU_DTYPE), _NT_DIMS,
                             preferred_element_type=jnp.float32)) / den

    def own_block(o_bd):
        out = jnp.zeros((nh, HEAD_DIM), jnp.float32)
        for g in range(KV_HEADS):
            out = jnp.where(row_group == g, o_bd[:, g * HEAD_DIM:(g + 1) * HEAD_DIM], out)
        return out

    gate = jax.nn.sigmoid(gate_ref[0])
    o_ref[0] = gate[:, 0:1] * o_c + gate[:, 1:2] * own_block(o_s) + gate[:, 2:3] * own_block(o_w)


def nsa_sample(q_r, q_n, zg, skv_new, wkv_new, cmp_pages, slc_pages, win_t, page_table, layer,
               cmp_pe, cmp_w1, cmp_b1, cmp_w2, kn):
    bs = q_r.shape[0]
    n_pages = page_table.shape[1]
    past_len = n_pages * PAGE_SIZE
    win_len = win_t.shape[-1]
    n_seg = n_pages * SEGS_PER_PAGE
    nb = n_seg - CMP_RATIO + 1
    ns = past_len // SLC_BLOCK + 1
    ns_pad = -(-ns // 128) * 128
    scale = HEAD_DIM ** -0.5
    qn = q_n.reshape(bs, N_HEADS, HEAD_DIM) * scale
    eye = jnp.eye(KV_HEADS, dtype=q_r.dtype)
    qr_bd = (q_r.reshape(bs, KV_HEADS, Q_PER_KV, 1, HEAD_DIM) * scale
             * eye[None, :, None, :, None]).reshape(bs, N_HEADS, GD)
    gate_in = zg.reshape(bs, N_HEADS, 3)
    new = jnp.concatenate([skv_new.reshape(bs, 2, GD), wkv_new.reshape(bs, 2, GD)], axis=1)
    w1r = cmp_w1.reshape(2, CMP_RATIO, SEG_WIDTH, HEAD_DIM)
    w1c = jnp.concatenate([w1r[:, r] for r in range(CMP_RATIO)], axis=-1).astype(MXU_DTYPE)
    b0 = (jnp.einsum('cld,cldh->ch', cmp_pe, cmp_w1, precision=lax.Precision.HIGHEST) + cmp_b1)[:, None, :]
    ovl = np.zeros((n_seg, ns_pad), np.float32)
    ovl[:nb, :ns] = overlap_matrix_np(nb, ns)
    eexp = (np.arange(past_len)[None, :] // SLC_BLOCK == np.arange(ns_pad)[:, None]).astype(np.float32)
    const = lambda shape: pl.BlockSpec(shape, lambda b, pt: (0,) * len(shape))
    per_b = lambda shape: pl.BlockSpec((1,) + shape, lambda b, pt: (b,) + (0,) * len(shape))
    cmp_spec = [pl.BlockSpec((1, 1, 2, KV_HEADS, SEGS_PER_PAGE, SEG_WIDTH),
                             functools.partial(lambda b, pt, pg: (layer, pt[b, pg], 0, 0, 0, 0), pg=pg))
                for pg in range(n_pages)]
    slc_spec = [pl.BlockSpec((1, 1, 2 * GD, PAGE_SIZE),
                             functools.partial(lambda b, pt, pg: (layer, pt[b, pg], 0, 0), pg=pg))
                for pg in range(n_pages)]
    grid_spec = pltpu.PrefetchScalarGridSpec(
        num_scalar_prefetch=1,
        grid=(bs,),
        in_specs=[per_b((N_HEADS, HEAD_DIM)), per_b((N_HEADS, GD)), per_b((N_HEADS, 3)), per_b((4, GD)),
                  const((2, SEG_WIDTH, CMP_RATIO * HEAD_DIM)), const((2, 1, HEAD_DIM)),
                  const((2, HEAD_DIM, HEAD_DIM)), const((1, HEAD_DIM)),
                  const((n_seg, ns_pad)), const((ns_pad, past_len)),
                  pl.BlockSpec((1, 1, 2 * GD, win_len), lambda b, pt: (layer, b, 0, 0))]
                 + cmp_spec + slc_spec,
        out_specs=per_b((N_HEADS, HEAD_DIM)),
    )
    out = pl.pallas_call(
        functools.partial(_nsa_sample_kernel, n_pages=n_pages, past_len=past_len, win_len=win_len),
        out_shape=jax.ShapeDtypeStruct((bs, N_HEADS, HEAD_DIM), jnp.float32),
        grid_spec=grid_spec,
        compiler_params=pltpu.CompilerParams(
            dimension_semantics=("arbitrary",),
            vmem_limit_bytes=VMEM_LIMIT_BYTES),
        name="nsa_sample",
    )(page_table, qn, qr_bd, gate_in, new, w1c, b0, cmp_w2, kn.reshape(1, HEAD_DIM),
      jnp.asarray(ovl, MXU_DTYPE), jnp.asarray(eexp, MXU_DTYPE), win_t,
      *([cmp_pages] * n_pages), *([slc_pages] * n_pages))
    return out.reshape(bs, 1, BRANCH_WIDTH)


def gmlp_mix(u, v, w_s, b_s):
    b, t, _ = v.shape
    nc = -(-t // CHUNK)
    vp = jnp.pad(v, ((0, 0), (0, nc * CHUNK - t), (0, 0))).reshape(b, nc, CHUNK, GMLP_GROUPS, GMLP_WIDTH // GMLP_GROUPS)
    w = w_s * jnp.tril(jnp.ones((CHUNK, CHUNK), w_s.dtype))
    mixed = jnp.einsum('gpq,bcqgd->bcpgd', w, vp) + b_s.T[None, None, :, :, None]
    return u * mixed.reshape(b, nc * CHUNK, GMLP_WIDTH)[:, :t]


def conformer_conv(zglu, buf, w_dw, b_dw, norm_g):
    a, gt = jnp.split(zglu, 2, axis=-1)
    glu = a * jax.nn.sigmoid(gt)
    xin = jnp.concatenate([buf.astype(glu.dtype), glu], axis=1)
    y = lax.conv_general_dilated(xin, w_dw[:, None, :].astype(glu.dtype), (1,), 'VALID',
                                 dimension_numbers=('NWC', 'WIO', 'NWC'), feature_group_count=CONV_WIDTH) + b_dw
    y = jax.nn.silu(rms_norm(y, norm_g))
    return y, xin[:, xin.shape[1] - (CONV_K - 1):]


MERGE_TN = 512


def _merge_kernel(h_ref, o0_ref, o1_ref, o2_ref, wm0_ref, wm1_ref, wm2_ref, bm0_ref, bm1_ref, bm2_ref,
                  wb_ref, y_ref):
    h = h_ref[...].astype(MXU_DTYPE)
    acc = None
    for i, (o_ref, wm_ref, bm_ref) in enumerate(((o0_ref, wm0_ref, bm0_ref), (o1_ref, wm1_ref, bm1_ref),
                                                 (o2_ref, wm2_ref, bm2_ref))):
        mg = jax.nn.sigmoid(jnp.dot(h, wm_ref[...].astype(MXU_DTYPE), preferred_element_type=jnp.float32)
                            + bm_ref[...])
        term = mg * jnp.dot(o_ref[...].astype(MXU_DTYPE), wb_ref[i].astype(MXU_DTYPE),
                            preferred_element_type=jnp.float32)
        acc = term if acc is None else acc + term
    y_ref[...] = acc.astype(y_ref.dtype)


def merge_branches(h, o_nsa, o_gmlp, o_conv, w_branch, w_merge, b_merge):
    m, d = h.shape
    tm = 512 if m % 512 == 0 else m
    tn = MERGE_TN
    nj = d // tn
    bw = o_nsa.shape[1]
    b2 = b_merge.reshape(1, 3 * d)
    o_spec = pl.BlockSpec((tm, bw), lambda i, j: (i, 0))
    wm_specs = [pl.BlockSpec((d, tn), functools.partial(lambda i, j, k: (0, k * nj + j), k=k)) for k in range(3)]
    bm_specs = [pl.BlockSpec((1, tn), functools.partial(lambda i, j, k: (0, k * nj + j), k=k)) for k in range(3)]
    return pl.pallas_call(
        _merge_kernel,
        out_shape=jax.ShapeDtypeStruct((m, d), MXU_DTYPE),
        grid=(m // tm, nj),
        in_specs=[pl.BlockSpec((tm, d), lambda i, j: (i, 0)), o_spec, o_spec, o_spec]
                 + wm_specs + bm_specs + [pl.BlockSpec((3, bw, tn), lambda i, j: (0, 0, j))],
        out_specs=pl.BlockSpec((tm, tn), lambda i, j: (i, j)),
        compiler_params=pltpu.CompilerParams(
            dimension_semantics=("parallel", "arbitrary"),
            vmem_limit_bytes=VMEM_LIMIT_BYTES),
        name="merge_branches",
    )(h, o_nsa, o_gmlp, o_conv, w_merge, w_merge, w_merge, b2, b2, b2, w_branch)


def mixer_merge(x, h, o_nsa, o_gmlp, o_conv, gate, w_branch, w_merge, b_merge, w_out):
    b, t, d = x.shape
    flat = lambda a: a.reshape(b * t, a.shape[-1]).astype(MXU_DTYPE)
    y = merge_branches(flat(h), flat(o_nsa), flat(o_gmlp), flat(o_conv), w_branch, w_merge, b_merge)
    return x + gate[:, None, :] * mm(y, w_out).reshape(b, t, d)


PEER_TB = 512
PEER_I1 = 8
PEER_JSUB = 4
PEER_KSUB = 32
PEER_NEG = -3.0e38
PEER_NRANK = PEER_TOPK + 1
PEER_VROWS = 24
PEER_NCAND = PEER_VROWS + 7 * 8 + 16


def _extract_top(cur_ref, out_ref, n_out):
    n = cur_ref.shape[0]
    iota = lax.broadcasted_iota(jnp.int32, cur_ref.shape, 0)
    for k in range(n_out):
        cur = cur_ref[...]
        m = jnp.max(cur, axis=0, keepdims=True)
        first = jnp.min(jnp.where(cur == m, iota, n), axis=0, keepdims=True)
        cur_ref[...] = jnp.where(iota == first, PEER_NEG, cur)
        out_ref[k:k + 1, :] = m


def _peer_route_kernel(x_ref, w_ref, k_ref, thr1_ref, e1_ref, s2_ref, e2_ref,
                       work_ref, v1_ref, v2_ref, cand_ref, ctop_ref):
    x = x_ref[...].astype(MXU_DTYPE)
    w = w_ref[...].astype(MXU_DTYPE)
    q = jnp.dot(x, w, preferred_element_type=jnp.float32)
    half = PEER_QDIM // 2
    nt = (((1,), (1,)), ((), ()))
    s = []
    for c in range(2):
        qc = q[:, c * half:(c + 1) * half].astype(MXU_DTYPE)
        kc = k_ref[0, c].astype(MXU_DTYPE)
        s.append(lax.dot_general(kc, qc, nt, preferred_element_type=jnp.float32))
    s1, s2 = s
    v1_ref[...] = jnp.full(v1_ref.shape, PEER_NEG, jnp.float32)
    v2_ref[...] = jnp.full(v2_ref.shape, PEER_NEG, jnp.float32)
    work_ref[...] = s1
    _extract_top(work_ref, v1_ref, PEER_NRANK)
    work_ref[...] = s2
    _extract_top(work_ref, v2_ref, PEER_NRANK)
    v1 = v1_ref[...]
    v2 = v2_ref[...]
    nv = PEER_VROWS
    cand_ref[0:nv, :] = v1[0:1] + v2
    for a in range(1, 8):
        cand_ref[nv + 8 * (a - 1):nv + 8 * a, :] = v1[a:a + 1] + v2[0:8]
    cand_ref[nv + 56:nv + 72, :] = v1[8:nv] + v2[0:1]
    _extract_top(cand_ref, ctop_ref, PEER_NRANK)
    tau = 0.5 * (ctop_ref[PEER_TOPK - 1:PEER_TOPK, :] + ctop_ref[PEER_TOPK:PEER_TOPK + 1, :])
    m1 = v1[0:1]
    m2 = v2[0:1]
    e1top = jnp.exp(v1 - m1)
    e2top = jnp.exp(v2 - m2)
    z = jnp.zeros_like(tau)
    for a in range(PEER_NRANK):
        sel = v2 >= (tau - v1[a:a + 1])
        z = z + jnp.sum(jnp.where(sel, e2top, 0.0), axis=0, keepdims=True) * e1top[a:a + 1]
    inv_z = 1.0 / z
    thr1_ref[0] = tau - s1
    e1_ref[0] = jnp.exp(s1 - m1) * inv_z
    s2_ref[0] = s2
    e2_ref[0] = jnp.exp(s2 - m2)


def peer_route(xm, w_pq, sub_keys):
    t, d = xm.shape
    tb = PEER_TB
    out = jax.ShapeDtypeStruct((PEER_HEADS, PEER_KEYS, t), jnp.float32)
    ospec = pl.BlockSpec((1, PEER_KEYS, tb), lambda i, h: (h, 0, i))
    return pl.pallas_call(
        _peer_route_kernel,
        out_shape=(out, out, out, out),
        grid=(t // tb, PEER_HEADS),
        in_specs=[pl.BlockSpec((tb, d), lambda i, h: (i, 0)),
                  pl.BlockSpec((d, PEER_QDIM), lambda i, h: (0, h)),
                  pl.BlockSpec((1, 2, PEER_KEYS, PEER_QDIM // 2), lambda i, h: (h, 0, 0, 0))],
        out_specs=(ospec, ospec, ospec, ospec),
        scratch_shapes=[pltpu.VMEM((PEER_KEYS, tb), jnp.float32),
                        pltpu.VMEM((PEER_VROWS, tb), jnp.float32),
                        pltpu.VMEM((PEER_VROWS, tb), jnp.float32),
                        pltpu.VMEM((PEER_NCAND, tb), jnp.float32),
                        pltpu.VMEM((PEER_VROWS, tb), jnp.float32)],
        compiler_params=pltpu.CompilerParams(
            dimension_semantics=("parallel", "arbitrary"),
            vmem_limit_bytes=VMEM_LIMIT_BYTES),
        name="peer_route",
    )(xm, w_pq, sub_keys)


def _peer_dense_kernel(xt_ref, u_ref, vt_ref, thr1_ref, e1_ref, s2_ref, e2_ref, o_ref, s_cur, w_new):
    @pl.when(pl.program_id(1) == 0)
    def _():
        o_ref[...] = jnp.zeros_like(o_ref)

    tb = o_ref.shape[1]
    nk = PEER_KEYS
    ks = PEER_KSUB
    s_cur[...] = jnp.dot(u_ref[...], xt_ref[...], preferred_element_type=jnp.float32)
    for t0 in range(0, tb, 128):
        lanes = slice(t0, t0 + 128)
        for k0 in range(0, nk, ks):
            for j0 in range(0, PEER_I1, PEER_JSUB):
                g = [jnp.zeros((ks, 128), jnp.float32) for _ in range(PEER_JSUB)]
                for h in range(PEER_HEADS):
                    s2 = s2_ref[h, k0:k0 + ks, lanes]
                    e2 = e2_ref[h, k0:k0 + ks, lanes]
                    for jj in range(PEER_JSUB):
                        j = j0 + jj
                        sel = s2 >= thr1_ref[h, j:j + 1, lanes]
                        g[jj] = g[jj] + jnp.where(sel, e2, 0.0) * e1_ref[h, j:j + 1, lanes]
                for jj in range(PEER_JSUB):
                    r0 = (j0 + jj) * nk + k0
                    act = jax.nn.gelu(s_cur[r0:r0 + ks, lanes])
                    w_new[r0:r0 + ks, lanes] = (act * g[jj]).astype(MXU_DTYPE)
    o_ref[...] += lax.dot_general(vt_ref[...], w_new[...], (((0,), (0,)), ((), ())),
                                  preferred_element_type=jnp.float32)


def peer_dense(xt, u_bf, vt_bf, thr1, e1, s2, e2):
    d, t = xt.shape
    n_exp = u_bf.shape[0]
    tb = PEER_TB
    ec = PEER_I1 * PEER_KEYS
    row_spec = pl.BlockSpec((PEER_HEADS, PEER_I1, tb), lambda i, c: (0, c, i))
    full_spec = pl.BlockSpec((PEER_HEADS, PEER_KEYS, tb), lambda i, c: (0, 0, i))
    return pl.pallas_call(
        _peer_dense_kernel,
        out_shape=jax.ShapeDtypeStruct((d, t), jnp.float32),
        grid=(t // tb, n_exp // ec),
        in_specs=[pl.BlockSpec((d, tb), lambda i, c: (0, i)),
                  pl.BlockSpec((ec, d), lambda i, c: (c, 0)),
                  pl.BlockSpec((ec, d), lambda i, c: (c, 0)),
                  row_spec, row_spec, full_spec, full_spec],
        out_specs=pl.BlockSpec((d, tb), lambda i, c: (0, i)),
        scratch_shapes=[pltpu.VMEM((ec, tb), jnp.float32), pltpu.VMEM((ec, tb), MXU_DTYPE)],
        compiler_params=pltpu.CompilerParams(
            dimension_semantics=("parallel", "arbitrary"),
            vmem_limit_bytes=VMEM_LIMIT_BYTES),
        name="peer_dense",
    )(xt, u_bf, vt_bf, thr1, e1, s2, e2)


def peer_ffn(xm, w_pq, sub_keys, u_bf, vt_bf):
    t = sum(part.shape[0] for part in xm)
    d = xm[0].shape[1]
    tp = -(-t // PEER_TB) * PEER_TB
    xm_p = jnp.concatenate(list(xm) + [jnp.zeros((tp - t, d), xm[0].dtype)], axis=0)
    thr1, e1, s2, e2 = peer_route(xm_p, w_pq, sub_keys)
    out_t = peer_dense(xm_p.T.astype(MXU_DTYPE), u_bf, vt_bf, thr1, e1, s2, e2)
    return out_t.T[:t]


def kernel(x_prompt, x_sample, cache_cmp_kv, cache_slc_kv, state_win_kv, state_conv, page_table,
           c_prompt, c_sample, w_ada, b_ada, norm_mix, norm_ffn, w_in, q_norm, k_norm,
           cmp_pe, cmp_w1, cmp_b1, cmp_w2, gmlp_norm, gmlp_ws, gmlp_bs, conv_w, conv_b, conv_norm,
           w_branch, w_merge, b_merge, w_out, peer_wq, peer_keys, peer_u, peer_v):
    bp, t = x_prompt.shape[:2]
    bs, ds = x_sample.shape[:2]
    depth = w_in.shape[0]
    past_len = page_table.shape[1] * PAGE_SIZE
    pos_p = jnp.arange(t, dtype=jnp.int32)
    pos_s = past_len + jnp.arange(ds, dtype=jnp.int32)
    xp, xs = x_prompt, x_sample
    assert ds == 1
    n_phys = cache_cmp_kv.shape[1]
    cmp_pages = cache_cmp_kv.reshape(depth, n_phys, SEGS_PER_PAGE, CMP_STRIDE, 2, KV_HEADS, HEAD_DIM)
    cmp_pages = cmp_pages.transpose(0, 1, 4, 5, 2, 3, 6).reshape(
        depth, n_phys, 2, KV_HEADS, SEGS_PER_PAGE, SEG_WIDTH)
    slc_pages = cache_slc_kv.transpose(0, 1, 3, 4, 5, 2).reshape(depth, n_phys, 2 * GD, PAGE_SIZE)
    win_t = state_win_kv.transpose(0, 1, 3, 4, 5, 2).reshape(depth, bs, 2 * GD, state_win_kv.shape[2])
    cmp_p, cmp_s, slc_p, slc_s, win_p, win_s, conv_p, conv_s, gv_s = [], [], [], [], [], [], [], [], []
    for l in range(depth):
        mp, msm = adaln(c_prompt, c_sample, w_ada[l], b_ada[l])
        w_in_l, w_merge_l, w_branch_l, w_out_l = (w.astype(MXU_DTYPE) for w in
                                                  (w_in[l], w_merge[l], w_branch[l], w_out[l]))
        hp, q_r, q_n, gates, zg, ckv, skv, wkv, u, v, zglu = mixer_front(
            xp, mp[0], mp[1], norm_mix[l], w_in_l, q_norm[l], k_norm[l], gmlp_norm[l], pos_p)
        kc, vc, c_end = compress(ckv, cmp_pe[l], cmp_w1[l], cmp_b1[l], cmp_w2[l], k_norm[l, 0])
        o_nsa = nsa_prompt(q_r, q_n, zg, kc, vc, skv, wkv)
        o_g = gmlp_mix(u, v, gmlp_ws[l], gmlp_bs[l])
        o_c, buf = conformer_conv(zglu, jnp.zeros((bp, CONV_K - 1, CONV_WIDTH), zglu.dtype),
                                  conv_w[l], conv_b[l], conv_norm[l])
        xp = mixer_merge(xp, hp, o_nsa, o_g, o_c, mp[2], w_branch_l, w_merge_l, b_merge[l], w_out_l)
        cmp_p.append(ckv)
        slc_p.append(skv)
        win_p.append(wkv[:, t - min(WINDOW, t):])
        conv_p.append(buf)
        hs, q_r, q_n, gates, zg, ckv, skv, wkv, u, v, zglu = mixer_front(
            xs, msm[0], msm[1], norm_mix[l], w_in_l, q_norm[l], k_norm[l], gmlp_norm[l], pos_s)
        win_full = jnp.concatenate([state_win_kv[l], wkv], axis=1)
        o_nsa = nsa_sample(q_r, q_n, zg, skv, wkv, cmp_pages, slc_pages, win_t, page_table, l,
                           cmp_pe[l], cmp_w1[l], cmp_b1[l], cmp_w2[l], k_norm[l, 0])
        o_g = gmlp_mix(u, v, gmlp_ws[l], gmlp_bs[l])
        o_c, buf = conformer_conv(zglu, state_conv[l], conv_w[l], conv_b[l], conv_norm[l])
        xs = mixer_merge(xs, hs, o_nsa, o_g, o_c, msm[2], w_branch_l, w_merge_l, b_merge[l], w_out_l)
        d = xp.shape[-1]
        xm_all = (modulate(xp, norm_ffn[l], mp[3], mp[4]).reshape(bp * t, d),
                  modulate(xs, norm_ffn[l], msm[3], msm[4]).reshape(bs * ds, d))
        ffn = peer_ffn(xm_all, peer_wq[l].astype(MXU_DTYPE), peer_keys[l],
                       peer_u[l].astype(MXU_DTYPE), peer_v[l].astype(MXU_DTYPE))
        xp = xp + mp[5][:, None, :] * ffn[:bp * t].reshape(bp, t, d)
        xs = xs + msm[5][:, None, :] * ffn[bp * t:].reshape(bs, ds, d)
        cmp_s.append(ckv)
        slc_s.append(skv)
        lw = win_full.shape[1]
        win_s.append(win_full[:, lw - min(WINDOW, lw):])
        conv_s.append(buf)
        gv_s.append(v)
    return (xp, xs, jnp.stack(cmp_p), jnp.stack(cmp_s), jnp.stack(slc_p), jnp.stack(slc_s),
            jnp.stack(win_p), jnp.stack(win_s), jnp.stack(conv_p), jnp.stack(conv_s), jnp.stack(gv_s))
```
